```python
import math
import jax
import jax.numpy as jnp
from jax import lax
import numpy as np

D_MODEL = 2048
BATCH = 8
SEQ = 2048
DEPTH = 1
DEC_BATCH = 32
DEC_SEQ = 8
PAST_LEN = 8192
PAGE_SIZE = 128

D_PLE = 256
SB_HEADS = 8
SB_DIM = D_MODEL // 16
SB_WIDTH = SB_HEADS * SB_DIM
DF_HEADS = 4
DF_DIM = D_MODEL // 16
DF_WIDTH = DF_HEADS * 2 * DF_DIM
N_EXPERTS = 32
TOP_K = 4
D_FF = D_MODEL
SWIGLU_LIMIT = 7.0
SWIGLU_ALPHA = 1.702
Q_BLOCK = 128
MOE_BLOCK = 256
RMS_EPS = 1e-6
SB_SCALE = 1.0 / math.sqrt(SB_DIM)
DF_SCALE = 1.0 / math.sqrt(DF_DIM)
IN_WIDTHS = (SB_WIDTH, SB_WIDTH, SB_WIDTH, DF_WIDTH, DF_WIDTH, DF_WIDTH, D_MODEL, D_MODEL)
IN_SPLITS = tuple(sum(IN_WIDTHS[:j + 1]) for j in range(len(IN_WIDTHS) - 1))
N_IN = sum(IN_WIDTHS)

kernel_name = "hybrid_stickbreak_diffattn_moe_decode_step"


def lambda_init(i):
    return 0.8 - 0.6 * math.exp(-0.3 * i)


def rmsnorm(x, g):
    xf = x.astype(jnp.float32)
    y = xf * lax.rsqrt(jnp.mean(xf * xf, axis=-1, keepdims=True) + RMS_EPS)
    return (y * g.astype(jnp.float32)).astype(x.dtype)


def stick_breaking(q, k, v, q_pos, k_pos):
    z = jnp.einsum('bqhd,bkhd->bhqk', q, k).astype(jnp.float32) * SB_SCALE
    mask = k_pos[None, :] < q_pos[:, None]
    log_fail = jnp.where(mask, jax.nn.log_sigmoid(-z), 0.0)
    after = lax.cumsum(log_fail, axis=3, reverse=True) - log_fail
    a = jnp.where(mask, jnp.exp(jax.nn.log_sigmoid(z) + after), 0.0)
    return jnp.einsum('bhqk,bkhd->bqhd', a.astype(v.dtype), v)


def diff_attention(q, k, v, q_pos, k_pos, lam):
    s = jnp.einsum('bqhmd,bkhmd->bhmqk', q, k).astype(jnp.float32) * DF_SCALE
    slopes = jnp.exp2(-8.0 * jnp.arange(1, DF_HEADS + 1, dtype=jnp.float32) / DF_HEADS)
    dist = (q_pos[:, None] - k_pos[None, :]).astype(jnp.float32)
    s = s - slopes[:, None, None, None] * dist
    s = jnp.where(k_pos[None, :] <= q_pos[:, None], s, -jnp.inf)
    p = jax.nn.softmax(s, axis=-1)
    w = p[:, :, 0] - lam * p[:, :, 1]
    return jnp.einsum('bhqk,bkhe->bqhe', w.astype(v.dtype), v)


def prompt_attention(q_sb, k_sb, v_sb, q_df, k_df, v_df, lam):
    b, t = q_sb.shape[:2]
    pos = jnp.arange(t)

    def one_block(start):
        qp = start + jnp.arange(Q_BLOCK)
        qs = lax.dynamic_slice_in_dim(q_sb, start, Q_BLOCK, axis=1)
        qd = lax.dynamic_slice_in_dim(q_df, start, Q_BLOCK, axis=1)
        return (stick_breaking(qs, k_sb, v_sb, qp, pos),
                diff_attention(qd, k_df, v_df, qp, pos, lam))

    o_sb, o_df = lax.map(one_block, jnp.arange(0, t, Q_BLOCK))
    o_sb = jnp.moveaxis(o_sb, 0, 1).reshape(b, t, SB_HEADS, SB_DIM)
    o_df = jnp.moveaxis(o_df, 0, 1).reshape(b, t, DF_HEADS, 2 * DF_DIM)
    return o_sb, o_df


def sample_attention(cache_sb_i, cache_df_i, page_table, q_sb, k_sb, v_sb, q_df, k_df, v_df, lam):
    t = q_sb.shape[1]
    past = page_table.shape[1] * PAGE_SIZE
    q_pos = past + jnp.arange(t)
    k_pos = jnp.arange(past + t)

    def one_seq(args):
        pt, qs, ks, vs, qd, kd, vd = args
        sb = cache_sb_i[pt].reshape(past, 2, SB_HEADS, SB_DIM)
        df = cache_df_i[pt].reshape(past, 2, DF_HEADS, 2 * DF_DIM)
        k1 = jnp.concatenate([sb[:, 0], ks], axis=0)[None]
        v1 = jnp.concatenate([sb[:, 1], vs], axis=0)[None]
        k2 = jnp.concatenate([df[:, 0].reshape(past, DF_HEADS, 2, DF_DIM), kd], axis=0)[None]
        v2 = jnp.concatenate([df[:, 1], vd], axis=0)[None]
        return (stick_breaking(qs[None], k1, v1, q_pos, k_pos)[0],
                diff_attention(qd[None], k2, v2, q_pos, k_pos, lam)[0])

    return lax.map(one_seq, (page_table, q_sb, k_sb, v_sb, q_df, k_df, v_df))


def clamped_swiglu(gu):
    gate, up = jnp.split(gu, 2, axis=-1)
    gate = jnp.minimum(gate, SWIGLU_LIMIT)
    up = jnp.clip(up, -SWIGLU_LIMIT, SWIGLU_LIMIT)
    return (up + 1.0) * gate * jax.nn.sigmoid(SWIGLU_ALPHA * gate)


def moe_ffn(x, w_router, b_router, w_gate_up, b_gate_up, w_down, b_down):
    n_tok = x.shape[0]
    n_slot = n_tok * TOP_K
    avg = max(1, n_slot // N_EXPERTS)
    blk = min(MOE_BLOCK, 1 << (avg.bit_length() - 1))
    n_blocks = -(-(n_slot + N_EXPERTS * (blk - 1)) // blk)
    n_rows = n_blocks * blk
    logits = (x @ w_router + b_router).astype(jnp.float32)
    top_val, top_idx = lax.top_k(logits, TOP_K)
    gate = jax.nn.softmax(top_val, axis=-1)
    flat_e = top_idx.reshape(-1)
    flat_tok = jnp.repeat(jnp.arange(n_tok, dtype=jnp.int32), TOP_K)
    order = jnp.argsort(flat_e)
    e_sorted = flat_e[order]
    counts = jnp.bincount(flat_e, length=N_EXPERTS)
    starts = jnp.cumsum(counts) - counts
    padded = (counts + blk - 1) // blk * blk
    pad_ends = jnp.cumsum(padded)
    pad_starts = pad_ends - padded
    dest = pad_starts[e_sorted] + jnp.arange(n_slot) - starts[e_sorted]
    row_tok = jnp.zeros((n_rows,), jnp.int32).at[dest].set(flat_tok[order])
    row_w = jnp.zeros((n_rows,), x.dtype).at[dest].set(gate.reshape(-1)[order].astype(x.dtype))
    block_start = jnp.arange(0, n_rows, blk)
    block_e = jnp.minimum(jnp.sum(pad_ends[None, :] <= block_start[:, None], axis=1), N_EXPERTS - 1)

    def expert_block(args):
        e, tok = args
        gu = x[tok] @ w_gate_up[e] + b_gate_up[e]
        return clamped_swiglu(gu) @ w_down[e] + b_down[e]

    y = lax.map(expert_block, (block_e, row_tok.reshape(n_blocks, blk))).reshape(n_rows, -1)
    return jnp.zeros_like(x).at[row_tok].add(y * row_w[:, None])


def decoder_layer(h, ple, attend, i, P):
    b, t, _ = h.shape
    u = rmsnorm(h, P['norm_mix'][i])
    q_sb, k_sb, v_sb, q_df, k_df, v_df, g_sb, g_df = jnp.split(u @ P['w_in'][i], IN_SPLITS, axis=-1)
    q_sb = q_sb.reshape(b, t, SB_HEADS, SB_DIM)
    k_sb = k_sb.reshape(b, t, SB_HEADS, SB_DIM)
    v_sb = v_sb.reshape(b, t, SB_HEADS, SB_DIM)
    q_df = q_df.reshape(b, t, DF_HEADS, 2, DF_DIM)
    k_df = k_df.reshape(b, t, DF_HEADS, 2, DF_DIM)
    v_df = v_df.reshape(b, t, DF_HEADS, 2 * DF_DIM)
    lam_init = lambda_init(i)
    lq1, lk1, lq2, lk2 = P['df_lambda'][i].astype(jnp.float32)
    lam = jnp.exp(jnp.sum(lq1 * lk1)) - jnp.exp(jnp.sum(lq2 * lk2)) + lam_init
    o_sb, o_df = attend(i, q_sb, k_sb, v_sb, q_df, k_df, v_df, lam)
    o_df = rmsnorm(o_df, P['df_subln'][i]) * (1.0 - lam_init)
    br_sb = o_sb.reshape(b, t, SB_WIDTH) @ P['w_br_sb'][i]
    br_df = o_df.reshape(b, t, DF_WIDTH) @ P['w_br_df'][i]
    merged = jax.nn.sigmoid(g_sb) * br_sb + jax.nn.sigmoid(g_df) * br_df
    h = h + merged @ P['w_out'][i]
    u = rmsnorm(h, P['norm_moe'][i]).reshape(b * t, D_MODEL)
    h = h + moe_ffn(u, P['w_router'][i], P['b_router'][i], P['w_gate_up'][i], P['b_gate_up'][i],
                    P['w_down'][i], P['b_down'][i]).reshape(b, t, D_MODEL)
    u = rmsnorm(h, P['norm_ple'][i])
    h = h + jax.nn.sigmoid(u @ P['w_ple_gate'][i]) * (ple @ P['w_ple_proj'][i])
    kv_sb = jnp.stack([k_sb, v_sb], axis=2)
    kv_df = jnp.stack([k_df.reshape(b, t, DF_HEADS, 2 * DF_DIM), v_df], axis=2)
    return h, kv_sb, kv_df


def trunk(x, p, attend, P):
    h = x
    sb_rows, df_rows = [], []
    for i in range(DEPTH):
        h, kv_sb, kv_df = decoder_layer(h, p[i], attend, i, P)
        sb_rows.append(kv_sb)
        df_rows.append(kv_df)
    return rmsnorm(h, P['norm_final']), jnp.stack(sb_rows, axis=0), jnp.stack(df_rows, axis=0)


def setup_inputs(seed: int = 0) -> dict:
    key = jax.random.key(seed)
    ks = jax.random.split(key, 32)
    n_pages = PAST_LEN // PAGE_SIZE
    n_used = DEC_BATCH * n_pages
    n_pool = n_used + max(1, n_used // 4)
    f32 = jnp.float32

    def nrm(k, shape, scale=1.0):
        return jax.random.normal(k, shape, f32) * scale

    def gain(k, shape):
        return 1.0 + 0.01 * jax.random.normal(k, shape, f32)

    page_table = jax.random.permutation(ks[4], n_pool)[:n_used].reshape(DEC_BATCH, n_pages).astype(jnp.int32)
    return {
        'x_prompt': nrm(ks[0], (BATCH, SEQ, D_MODEL)),
        'x_sample': nrm(ks[1], (DEC_BATCH, DEC_SEQ, D_MODEL)),
        'cache_sb': nrm(ks[2], (DEPTH, n_pool, PAGE_SIZE, 2, SB_HEADS, SB_DIM)),
        'cache_df': nrm(ks[3], (DEPTH, n_pool, PAGE_SIZE, 2, DF_HEADS, 2 * DF_DIM)),
        'page_table': page_table,
        'p_prompt': nrm(ks[5], (DEPTH, BATCH, SEQ, D_PLE)),
        'p_sample': nrm(ks[6], (DEPTH, DEC_BATCH, DEC_SEQ, D_PLE)),
        'norm_mix': gain(ks[7], (DEPTH, D_MODEL)),
        'w_in': nrm(ks[8], (DEPTH, D_MODEL, N_IN), D_MODEL ** -0.5),
        'df_lambda': nrm(ks[9], (DEPTH, 4, DF_DIM), 0.1),
        'df_subln': gain(ks[10], (DEPTH, 2 * DF_DIM)),
        'w_br_sb': nrm(ks[11], (DEPTH, SB_WIDTH, D_MODEL), SB_WIDTH ** -0.5),
        'w_br_df': nrm(ks[12], (DEPTH, DF_WIDTH, D_MODEL), DF_WIDTH ** -0.5),
        'w_out': nrm(ks[13], (DEPTH, D_MODEL, D_MODEL), D_MODEL ** -0.5),
        'norm_moe': gain(ks[14], (DEPTH, D_MODEL)),
        'w_router': nrm(ks[15], (DEPTH, D_MODEL, N_EXPERTS), D_MODEL ** -0.5),
        'b_router': nrm(ks[16], (DEPTH, N_EXPERTS), 0.01),
        'w_gate_up': nrm(ks[17], (DEPTH, N_EXPERTS, D_MODEL, 2 * D_FF), D_MODEL ** -0.5),
        'b_gate_up': nrm(ks[18], (DEPTH, N_EXPERTS, 2 * D_FF), 0.01),
        'w_down': nrm(ks[19], (DEPTH, N_EXPERTS, D_FF, D_MODEL), D_FF ** -0.5),
        'b_down': nrm(ks[20], (DEPTH, N_EXPERTS, D_MODEL), 0.01),
        'norm_ple': gain(ks[21], (DEPTH, D_MODEL)),
        'w_ple_gate': nrm(ks[22], (DEPTH, D_MODEL, D_MODEL), D_MODEL ** -0.5),
        'w_ple_proj': nrm(ks[23], (DEPTH, D_PLE, D_MODEL), D_PLE ** -0.5),
        'norm_final': gain(ks[24], (D_MODEL,)),
    }


def reference(x_prompt, x_sample, cache_sb, cache_df, page_table, p_prompt, p_sample,
              norm_mix, w_in, df_lambda, df_subln, w_br_sb, w_br_df, w_out, norm_moe,
              w_router, b_router, w_gate_up, b_gate_up, w_down, b_down, norm_ple,
              w_ple_gate, w_ple_proj, norm_final):
    P = dict(norm_mix=norm_mix, w_in=w_in, df_lambda=df_lambda, df_subln=df_subln,
             w_br_sb=w_br_sb, w_br_df=w_br_df, w_out=w_out, norm_moe=norm_moe,
             w_router=w_router, b_router=b_router, w_gate_up=w_gate_up, b_gate_up=b_gate_up,
             w_down=w_down, b_down=b_down, norm_ple=norm_ple, w_ple_gate=w_ple_gate,
             w_ple_proj=w_ple_proj, norm_final=norm_final)

    def attend_prompt(i, q_sb, k_sb, v_sb, q_df, k_df, v_df, lam):
        return prompt_attention(q_sb, k_sb, v_sb, q_df, k_df, v_df, lam)

    def attend_sample(i, q_sb, k_sb, v_sb, q_df, k_df, v_df, lam):
        return sample_attention(cache_sb[i], cache_df[i], page_table,
                                q_sb, k_sb, v_sb, q_df, k_df, v_df, lam)

    y_prompt, sb_prompt, df_prompt = trunk(x_prompt, p_prompt, attend_prompt, P)
    y_sample, sb_sample, df_sample = trunk(x_sample, p_sample, attend_sample, P)
    return (y_prompt, y_sample, sb_prompt, df_prompt, sb_sample, df_sample)
```

```python
import functools
import math

import jax
import jax.numpy as jnp
from jax import lax
from jax.experimental import pallas as pl
from jax.experimental.pallas import tpu as pltpu

F32 = jnp.float32
BF16 = jnp.bfloat16

SB_HEADS = 8
DF_HEADS = 4
N_GROUPS = 8
TOP_K = 4
SWIGLU_LIMIT = 7.0
SWIGLU_ALPHA = 1.702
RMS_EPS = 1e-6
LANES = 128
NEG_BIG = -1e30
VMEM_LIMIT = 56 * 1024 * 1024


def _lambda_init(i):
    return 0.8 - 0.6 * math.exp(-0.3 * i)


def _cparams(sem, vmem=VMEM_LIMIT):
    return pltpu.CompilerParams(dimension_semantics=sem, vmem_limit_bytes=vmem)


def _rms(x, g):
    return x * lax.rsqrt(jnp.mean(x * x, axis=-1, keepdims=True) + RMS_EPS) * g


def _const_spec(shape):
    nd = len(shape)
    return pl.BlockSpec(shape, lambda *_: (0,) * nd, pipeline_mode=pl.Buffered(1))


def _pitch(chunks):
    return chunks + 4 if chunks % 8 == 0 else chunks


def _read_chunks(ref, lead, rows, chunks, pitch):
    return jnp.concatenate(
        [ref[lead + (pl.ds(c, rows, stride=pitch), slice(None))] for c in range(chunks)], axis=1)


def _write_chunks(ref, val, rows, chunks):
    for c in range(chunks):
        ref[pl.ds(c, rows, stride=chunks), :] = val[:, c * LANES:(c + 1) * LANES]


def _norm_proj_kernel(x_ref, g_ref, w_ref, qsb_ref, kvsb_ref, qdf_ref, kvdf_ref, gate_ref,
                      u_ref, *, sb_scale, df_scale):
    j = pl.program_id(1)

    @pl.when(j == 0)
    def _():
        u_ref[...] = _rms(x_ref[...], g_ref[...]).astype(BF16)

    acc = jnp.dot(u_ref[...], w_ref[...], preferred_element_type=F32)

    @pl.when(j == 0)
    def _():
        qsb_ref[...] = (acc * sb_scale).astype(BF16)

    @pl.when((j >= 1) & (j < 3))
    def _():
        kvsb_ref[...] = acc

    @pl.when(j == 3)
    def _():
        qdf_ref[...] = (acc * df_scale).astype(BF16)

    @pl.when((j >= 4) & (j < 6))
    def _():
        kvdf_ref[...] = acc

    @pl.when(j >= 6)
    def _():
        gate_ref[...] = jax.nn.sigmoid(acc).astype(BF16)


def _norm_proj(x, g, w, tm):
    n, d = x.shape
    tn = d // 2
    n_in = w.shape[1]
    assert n_in == 10 * tn and n % tm == 0
    dh = d // 16
    kern = functools.partial(_norm_proj_kernel, sb_scale=1.0 / math.sqrt(dh),
                             df_scale=1.0 / math.sqrt(dh))
    clip = lambda j, lo, n_t: jnp.clip(j - lo, 0, n_t - 1)
    return pl.pallas_call(
        kern,
        grid=(n // tm, 10),
        in_specs=[pl.BlockSpec((tm, d), lambda i, j: (i, 0)),
                  pl.BlockSpec((1, d), lambda i, j: (0, 0)),
                  pl.BlockSpec((d, tn), lambda i, j: (0, j))],
        out_specs=[pl.BlockSpec((tm, tn), lambda i, j: (i, 0)),
                   pl.BlockSpec((tm, tn), lambda i, j: (i, clip(j, 1, 2))),
                   pl.BlockSpec((tm, tn), lambda i, j: (i, 0)),
                   pl.BlockSpec((tm, tn), lambda i, j: (i, clip(j, 4, 2))),
                   pl.BlockSpec((tm, tn), lambda i, j: (i, clip(j, 6, 4)))],
        out_shape=[jax.ShapeDtypeStruct((n, tn), BF16),
                   jax.ShapeDtypeStruct((n, 2 * tn), F32),
                   jax.ShapeDtypeStruct((n, tn), BF16),
                   jax.ShapeDtypeStruct((n, 2 * tn), F32),
                   jax.ShapeDtypeStruct((n, 4 * tn), BF16)],
        scratch_shapes=[pltpu.VMEM((tm, d), BF16)],
        compiler_params=_cparams(("parallel", "arbitrary")),
        name="norm_proj",
    )(x, g, w)


def _softplus(z):
    return jnp.maximum(z, 0.0) + jnp.log(1.0 + jnp.exp(-jnp.abs(z)))


def _sb_block(q, k, v, tri, mask, carry, acc):
    z = lax.dot_general(q, k, (((1,), (1,)), ((), ())), preferred_element_type=F32)
    lf = jnp.where(mask, -_softplus(z), 0.0)
    hi = lf.astype(BF16)
    lo = (lf - hi.astype(F32)).astype(BF16)
    inner = (jnp.dot(hi, tri, preferred_element_type=F32)
             + jnp.dot(lo, tri, preferred_element_type=F32))
    a = jnp.where(mask, jnp.exp(z + lf + carry + inner), 0.0)
    acc = acc + jnp.dot(a.astype(BF16), v, preferred_element_type=F32)
    carry = carry + jnp.sum(lf, axis=-1, keepdims=True)
    return carry, acc


def _df_block(q, k, v, bias, mask, m, l, acc):
    s = lax.dot_general(q, k, (((1,), (1,)), ((), ())), preferred_element_type=F32)
    s = jnp.where(mask, s - bias, NEG_BIG)
    m_new = jnp.maximum(m, jnp.max(s, axis=-1, keepdims=True))
    alpha = jnp.exp(m - m_new)
    p = jnp.exp(s - m_new)
    l = alpha * l + jnp.sum(p, axis=-1, keepdims=True)
    acc = alpha * acc + jnp.dot(p.astype(BF16), v, preferred_element_type=F32)
    return m_new, l, acc


def _df_lambda(lam_ref, lam_init):
    lq1, lk1 = lam_ref[0:1, :], lam_ref[1:2, :]
    lq2, lk2 = lam_ref[2:3, :], lam_ref[3:4, :]
    s1 = jnp.sum(lq1 * lk1, axis=-1, keepdims=True)
    s2 = jnp.sum(lq2 * lk2, axis=-1, keepdims=True)
    return jnp.exp(s1) - jnp.exp(s2) + lam_init


def _tri(tk):
    r = lax.broadcasted_iota(jnp.int32, (tk, tk), 0)
    c = lax.broadcasted_iota(jnp.int32, (tk, tk), 1)
    return (r > c).astype(BF16)


def _sb_prompt_kernel(q_ref, k_ref, v_ref, o_ref, *, tq):
    i = pl.program_id(2)
    q = q_ref[0]
    dh = q.shape[-1]
    tri = _tri(tq)
    row = lax.broadcasted_iota(jnp.int32, (tq, tq), 0)
    col = lax.broadcasted_iota(jnp.int32, (tq, tq), 1)

    def body(n, c):
        carry, acc = c
        kb = i - n
        start = pl.multiple_of(kb * tq, tq)
        k = k_ref[0, pl.ds(start, tq), :].astype(BF16)
        v = v_ref[0, pl.ds(start, tq), :].astype(BF16)
        mask = (col - row) < n * tq
        return _sb_block(q, k, v, tri, mask, carry, acc)

    init = (jnp.zeros((tq, 1), F32), jnp.zeros((tq, dh), F32))
    _, acc = lax.fori_loop(0, i + 1, body, init)
    o_ref[0] = acc.astype(o_ref.dtype)


def _sb_prompt(q, kv, b, t, tq):
    dh = q.shape[-1] // SB_HEADS
    return pl.pallas_call(
        functools.partial(_sb_prompt_kernel, tq=tq),
        grid=(b, SB_HEADS, t // tq),
        in_specs=[pl.BlockSpec((1, tq, dh), lambda bb, h, i: (bb, i, h)),
                  pl.BlockSpec((1, t, dh), lambda bb, h, i: (bb, 0, h)),
                  pl.BlockSpec((1, t, dh), lambda bb, h, i: (bb, 0, SB_HEADS + h))],
        out_specs=pl.BlockSpec((1, tq, dh), lambda bb, h, i: (bb, i, h)),
        out_shape=jax.ShapeDtypeStruct(q.shape, BF16),
        compiler_params=_cparams(("parallel", "parallel", "arbitrary")),
        name="sb_prompt",
    )(q, kv, kv)


def _df_finish(o1, o2, lam, g, lam_init):
    o = o1 - lam * o2
    return _rms(o, g) * (1.0 - lam_init)


def _df_prompt_kernel(slope_ref, q_ref, k_ref, v_ref, lam_ref, g_ref, o_ref, *, tq, lam_init):
    h = pl.program_id(1)
    i = pl.program_id(2)
    dh = q_ref.shape[-1] // 2
    q1 = q_ref[0, :, :dh]
    q2 = q_ref[0, :, dh:]
    slope = slope_ref[0, h]
    row = lax.broadcasted_iota(jnp.int32, (tq, tq), 0)
    col = lax.broadcasted_iota(jnp.int32, (tq, tq), 1)
    rel = (row - col).astype(F32)

    def body(kb, c):
        m1, l1, a1, m2, l2, a2 = c
        start = pl.multiple_of(kb * tq, tq)
        k1 = k_ref[0, pl.ds(start, tq), :dh].astype(BF16)
        k2 = k_ref[0, pl.ds(start, tq), dh:].astype(BF16)
        v = v_ref[0, pl.ds(start, tq), :].astype(BF16)
        off = (i - kb) * tq
        mask = (col - row) <= off
        bias = slope * (rel + off.astype(F32))
        m1, l1, a1 = _df_block(q1, k1, v, bias, mask, m1, l1, a1)
        m2, l2, a2 = _df_block(q2, k2, v, bias, mask, m2, l2, a2)
        return m1, l1, a1, m2, l2, a2

    zc = jnp.zeros((tq, 1), F32)
    za = jnp.zeros((tq, 2 * dh), F32)
    init = (zc + NEG_BIG, zc, za, zc + NEG_BIG, zc, za)
    _, l1, a1, _, l2, a2 = lax.fori_loop(0, i + 1, body, init)
    lam = _df_lambda(lam_ref, lam_init)
    o_ref[0] = _df_finish(a1 / l1, a2 / l2, lam, g_ref[...], lam_init).astype(o_ref.dtype)


def _df_prompt(q, kv, slopes, df_lambda, g, b, t, tq, lam_init):
    w = q.shape[-1] // DF_HEADS
    return pl.pallas_call(
        functools.partial(_df_prompt_kernel, tq=tq, lam_init=lam_init),
        grid=(b, DF_HEADS, t // tq),
        in_specs=[pl.BlockSpec(memory_space=pltpu.SMEM),
                  pl.BlockSpec((1, tq, w), lambda bb, h, i: (bb, i, h)),
                  pl.BlockSpec((1, t, w), lambda bb, h, i: (bb, 0, h)),
                  pl.BlockSpec((1, t, w), lambda bb, h, i: (bb, 0, DF_HEADS + h)),
                  _const_spec(df_lambda.shape),
                  _const_spec(g.shape)],
        out_specs=pl.BlockSpec((1, tq, w), lambda bb, h, i: (bb, i, h)),
        out_shape=jax.ShapeDtypeStruct(q.shape, BF16),
        compiler_params=_cparams(("parallel", "parallel", "arbitrary")),
        name="df_prompt",
    )(slopes, q, kv, kv, df_lambda, g)


def _sample_attn_kernel(pt_ref, qsb_ref, qdf_ref, nsb_ref, ndf_ref, csb_ref, cdf_ref,
                        slope_ref, lam_ref, g_ref, osb_ref, odf_ref,
                        carry_ref, asb_ref, m_ref, l_ref, adf_ref,
                        *, n_pages, page, t_new, lam_init):
    j = pl.program_id(1)
    n_rows = qsb_ref.shape[1]
    width = qsb_ref.shape[2]
    past = n_pages * page
    tri = _tri(page)
    row = lax.broadcasted_iota(jnp.int32, (n_rows, page), 0)
    col = lax.broadcasted_iota(jnp.int32, (n_rows, page), 1)
    qpos = past + row % t_new

    @pl.when(j == 0)
    def _():
        carry_ref[...] = jnp.zeros_like(carry_ref)
        asb_ref[...] = jnp.zeros_like(asb_ref)
        m_ref[...] = jnp.full_like(m_ref, NEG_BIG)
        l_ref[...] = jnp.zeros_like(l_ref)
        adf_ref[...] = jnp.zeros_like(adf_ref)

    def step(ksb, vsb, kdf, vdf, kpos):
        carry, asb = _sb_block(qsb_ref[0], ksb, vsb, tri, kpos < qpos, carry_ref[...], asb_ref[...])
        carry_ref[...] = carry
        asb_ref[...] = asb
        bias = slope_ref[...] * (qpos - kpos).astype(F32)
        m, l, adf = _df_block(qdf_ref[0], kdf, vdf, bias, kpos <= qpos,
                              m_ref[...], l_ref[...], adf_ref[...])
        m_ref[...] = m
        l_ref[...] = l
        adf_ref[...] = adf

    def groups(ref2d, kv):
        per = 2 * N_GROUPS
        return jnp.concatenate(
            [ref2d[pl.ds(kv * N_GROUPS + g, page, stride=per), :] for g in range(N_GROUPS)],
            axis=1).astype(BF16)

    @pl.when(j == 0)
    def _():
        pad = jnp.zeros((page - t_new, width), F32)
        nsb = nsb_ref[...]
        ndf = ndf_ref[...]
        ksb = jnp.concatenate([nsb[:, :width], pad], axis=0).astype(BF16)
        vsb = jnp.concatenate([nsb[:, width:], pad], axis=0).astype(BF16)
        kdf = jnp.concatenate([ndf[:, :width], pad], axis=0).astype(BF16)
        vdf = jnp.concatenate([ndf[:, width:], pad], axis=0).astype(BF16)
        step(ksb, vsb, kdf, vdf, past + col)

    @pl.when(j > 0)
    def _():
        csb = csb_ref.at[0]
        cdf = cdf_ref.at[0]
        step(groups(csb, 0), groups(csb, 1), groups(cdf, 0), groups(cdf, 1),
             (n_pages - j) * page + col)

    @pl.when(j == n_pages)
    def _():
        dh = width // N_GROUPS
        asb = asb_ref[...]
        osb_ref[...] = jnp.concatenate(
            [asb[h * t_new:(h + 1) * t_new, h * dh:(h + 1) * dh] for h in range(SB_HEADS)], axis=1)
        adf = adf_ref[...] / l_ref[...]
        lam = _df_lambda(lam_ref, lam_init)
        outs = []
        for h in range(DF_HEADS):
            def pick(m, h=h):
                rows = adf[(m * DF_HEADS + h) * t_new:(m * DF_HEADS + h + 1) * t_new, :]
                return jnp.concatenate([rows[:, h * dh:(h + 1) * dh],
                                        rows[:, (DF_HEADS + h) * dh:(DF_HEADS + h + 1) * dh]],
                                       axis=1)
            outs.append(_df_finish(pick(0), pick(1), lam, g_ref[...], lam_init))
        odf_ref[...] = jnp.concatenate(outs, axis=1)


def _block_diag_q(q, t_new):
    n, width = q.shape
    b = n // t_new
    q3 = q.reshape(b, 1, t_new, width)
    grp_r = jnp.arange(N_GROUPS)[:, None, None]
    grp_c = (jnp.arange(width) // (width // N_GROUPS))[None, None, :]
    out = jnp.where(grp_r == grp_c, q3, jnp.zeros((), q.dtype))
    return out.reshape(b, N_GROUPS * t_new, width)


def _sample_attn(q_sb, kv_sb, q_df, kv_df, cache_sb, cache_df, page_table, slopes, df_lambda, g,
                 t_new, lam_init):
    n, width = q_sb.shape
    b = n // t_new
    dh = width // N_GROUPS
    n_pages = page_table.shape[1]
    n_pool, page = cache_sb.shape[0], cache_sb.shape[1]
    csb = cache_sb.reshape(n_pool, page * 2 * N_GROUPS, dh)
    cdf = cache_df.reshape(n_pool, page, 2, DF_HEADS, 2, dh).transpose(0, 1, 2, 4, 3, 5)
    cdf = cdf.reshape(n_pool, page * 2 * N_GROUPS, dh)
    half_major = lambda a, lead: a.reshape(lead + (DF_HEADS, 2, dh)).swapaxes(-2, -3).reshape(
        lead + (N_GROUPS * dh,))
    q_df = half_major(q_df, (n,))
    kv_df = half_major(kv_df, (n, 2)).reshape(n, 2 * width)
    n_rows = N_GROUPS * t_new
    slope_rows = jnp.tile(jnp.repeat(slopes.reshape(-1), t_new), 2).reshape(n_rows, 1)
    pt = page_table.reshape(-1).astype(jnp.int32)

    def page_idx(bb, j, pt_ref):
        return (pt_ref[bb * n_pages + n_pages - jnp.maximum(j, 1)], 0, 0)

    kern = functools.partial(_sample_attn_kernel, n_pages=n_pages, page=page, t_new=t_new,
                             lam_init=lam_init)
    grid_spec = pltpu.PrefetchScalarGridSpec(
        num_scalar_prefetch=1,
        grid=(b, n_pages + 1),
        in_specs=[pl.BlockSpec((1, n_rows, width), lambda bb, j, p: (bb, 0, 0)),
                  pl.BlockSpec((1, n_rows, width), lambda bb, j, p: (bb, 0, 0)),
                  pl.BlockSpec((t_new, 2 * width), lambda bb, j, p: (bb, 0)),
                  pl.BlockSpec((t_new, 2 * width), lambda bb, j, p: (bb, 0)),
                  pl.BlockSpec((1,) + csb.shape[1:], page_idx),
                  pl.BlockSpec((1,) + cdf.shape[1:], page_idx),
                  pl.BlockSpec((n_rows, 1), lambda bb, j, p: (0, 0)),
                  pl.BlockSpec(df_lambda.shape, lambda bb, j, p: (0, 0)),
                  pl.BlockSpec(g.shape, lambda bb, j, p: (0, 0))],
        out_specs=[pl.BlockSpec((t_new, width), lambda bb, j, p: (bb, 0)),
                   pl.BlockSpec((t_new, width), lambda bb, j, p: (bb, 0))],
        scratch_shapes=[pltpu.VMEM((n_rows, 1), F32), pltpu.VMEM((n_rows, width), F32),
                        pltpu.VMEM((n_rows, 1), F32), pltpu.VMEM((n_rows, 1), F32),
                        pltpu.VMEM((n_rows, width), F32)])
    return pl.pallas_call(
        kern,
        grid_spec=grid_spec,
        out_shape=[jax.ShapeDtypeStruct((n, width), F32), jax.ShapeDtypeStruct((n, width), F32)],
        compiler_params=_cparams(("parallel", "arbitrary")),
        name="sample_attn",
    )(pt, _block_diag_q(q_sb, t_new), _block_diag_q(q_df, t_new), kv_sb, kv_df, csb, cdf,
      slope_rows, df_lambda, g)


def _post_attn_kernel(osb_ref, odf_ref, gate_ref, x_ref, wsb_ref, wdf_ref, wout_ref, gmoe_ref,
                      wr_ref, br_ref, h_ref, u_ref, idx_ref, gw_ref, *, n_experts):
    d = x_ref.shape[-1]
    br_sb = jnp.dot(osb_ref[...].astype(BF16), wsb_ref[...], preferred_element_type=F32)
    br_df = jnp.dot(odf_ref[...].astype(BF16), wdf_ref[...], preferred_element_type=F32)
    gate = gate_ref[...].astype(F32)
    merged = gate[:, :d] * br_sb + gate[:, d:] * br_df
    h = x_ref[...] + jnp.dot(merged.astype(BF16), wout_ref[...], preferred_element_type=F32)
    h_ref[...] = h
    u = _rms(h, gmoe_ref[...])
    _write_chunks(u_ref, u, u.shape[0], d // LANES)
    logits = jnp.dot(u, wr_ref[...], preferred_element_type=F32,
                     precision=lax.Precision.HIGHEST) + br_ref[...]
    tm = logits.shape[0]
    lane = lax.broadcasted_iota(jnp.int32, (tm, n_experts), 1)
    out_lane = lax.broadcasted_iota(jnp.int32, (tm, LANES), 1)
    idx_out = jnp.zeros((tm, LANES), jnp.int32)
    val_out = jnp.zeros((tm, LANES), F32)
    top = None
    denom = jnp.zeros((tm, 1), F32)
    for k in range(TOP_K):
        mx = jnp.max(logits, axis=-1, keepdims=True)
        idx = jnp.min(jnp.where(logits == mx, lane, n_experts), axis=-1, keepdims=True)
        logits = jnp.where(lane == idx, -jnp.inf, logits)
        if top is None:
            top = mx
        e = jnp.exp(mx - top)
        denom = denom + e
        idx_out = jnp.where(out_lane == k, idx, idx_out)
        val_out = jnp.where(out_lane == k, e, val_out)
    idx_ref[...] = idx_out
    gw_ref[...] = val_out / denom


def _post_attn(o_sb, o_df, gate, x, w_sb, w_df, w_out, g_moe, w_router, b_router, tm):
    n, d = x.shape
    e = w_router.shape[1]
    row = lambda w: pl.BlockSpec((tm, w), lambda i: (i, 0))
    return pl.pallas_call(
        functools.partial(_post_attn_kernel, n_experts=e),
        grid=(n // tm,),
        in_specs=[row(o_sb.shape[1]), row(o_df.shape[1]), row(gate.shape[1]), row(d),
                  _const_spec(w_sb.shape), _const_spec(w_df.shape), _const_spec(w_out.shape),
                  _const_spec(g_moe.shape), _const_spec(w_router.shape),
                  _const_spec(b_router.shape)],
        out_specs=[row(d),
                   pl.BlockSpec((tm * (d // LANES), LANES), lambda i: (i, 0)),
                   row(LANES), row(LANES)],
        out_shape=[jax.ShapeDtypeStruct((n, d), F32),
                   jax.ShapeDtypeStruct((n * (d // LANES), LANES), F32),
                   jax.ShapeDtypeStruct((n, LANES), jnp.int32),
                   jax.ShapeDtypeStruct((n, LANES), F32)],
        compiler_params=_cparams(("parallel",)),
        name="post_attn",
    )(o_sb, o_df, gate, x, w_sb, w_df, w_out, g_moe, w_router, b_router)


def _row_copy(src_hbm, src_row, dst, dst_slot, chunks, pitch, sem):
    src = src_hbm.at[pl.ds(pl.multiple_of(src_row * chunks, chunks), chunks), :]
    return pltpu.make_async_copy(src, dst.at[pl.ds(dst_slot * pitch, chunks), :], sem)


def _moe_gather_kernel(tok_ref, nu_ref, u_hbm, o_ref, buf, sem, *, tm, chunks, pitch):
    r = pl.program_id(0)

    @pl.when(r < nu_ref[0])
    def _():
        def copy(t):
            return _row_copy(u_hbm, tok_ref[r * tm + t], buf, t, chunks, pitch, sem)

        def issue(t, c):
            copy(t).start()
            return c

        def drain(t, c):
            copy(t).wait()
            return c

        lax.fori_loop(0, tm, issue, 0)
        lax.fori_loop(0, tm, drain, 0)
        o_ref[...] = _read_chunks(buf, (), tm, chunks, pitch).astype(BF16)

    @pl.when(r >= nu_ref[0])
    def _():
        o_ref[...] = jnp.zeros_like(o_ref)


def _moe_gather(row_tok, n_used, u3, chunks, n_rows, tm):
    pitch = _pitch(chunks)
    grid_spec = pltpu.PrefetchScalarGridSpec(
        num_scalar_prefetch=2,
        grid=(n_rows // tm,),
        in_specs=[pl.BlockSpec(memory_space=pl.ANY)],
        out_specs=pl.BlockSpec((tm, chunks * LANES), lambda r, *_: (r, 0)),
        scratch_shapes=[pltpu.VMEM((tm * pitch, LANES), F32), pltpu.SemaphoreType.DMA(())])
    return pl.pallas_call(
        functools.partial(_moe_gather_kernel, tm=tm, chunks=chunks, pitch=pitch),
        grid_spec=grid_spec,
        out_shape=jax.ShapeDtypeStruct((n_rows, chunks * LANES), BF16),
        compiler_params=_cparams(("arbitrary",)),
        name="moe_gather",
    )(row_tok, n_used, u3)


def _moe_up_kernel(be_ref, nu_ref, x_ref, wg_ref, wu_ref, bg_ref, bu_ref, h_ref):
    r = pl.program_id(0)

    @pl.when(r < nu_ref[0])
    def _():
        x = x_ref[...]
        gate = jnp.dot(x, wg_ref[0], preferred_element_type=F32) + bg_ref[0]
        up = jnp.dot(x, wu_ref[0], preferred_element_type=F32) + bu_ref[0]
        gate = jnp.minimum(gate, SWIGLU_LIMIT)
        up = jnp.clip(up, -SWIGLU_LIMIT, SWIGLU_LIMIT)
        h_ref[...] = ((up + 1.0) * gate * jax.nn.sigmoid(SWIGLU_ALPHA * gate)).astype(BF16)

    @pl.when(r >= nu_ref[0])
    def _():
        h_ref[...] = jnp.zeros_like(h_ref)


def _moe_up(block_e, n_used, xg, w_gu, b_gu, tm):
    n_rows, d = xg.shape
    f = w_gu.shape[2] // 2
    grid_spec = pltpu.PrefetchScalarGridSpec(
        num_scalar_prefetch=2,
        grid=(n_rows // tm,),
        in_specs=[pl.BlockSpec((tm, d), lambda r, be, nu: (r, 0)),
                  pl.BlockSpec((1, d, f), lambda r, be, nu: (be[r], 0, 0)),
                  pl.BlockSpec((1, d, f), lambda r, be, nu: (be[r], 0, 1)),
                  pl.BlockSpec((1, 1, f), lambda r, be, nu: (be[r], 0, 0)),
                  pl.BlockSpec((1, 1, f), lambda r, be, nu: (be[r], 0, 1))],
        out_specs=pl.BlockSpec((tm, f), lambda r, be, nu: (r, 0)))
    return pl.pallas_call(
        _moe_up_kernel,
        grid_spec=grid_spec,
        out_shape=jax.ShapeDtypeStruct((n_rows, f), BF16),
        compiler_params=_cparams(("arbitrary",)),
        name="moe_up",
    )(block_e, n_used, xg, w_gu, w_gu, b_gu, b_gu)


def _moe_down_kernel(be_ref, nu_ref, h_ref, wd_ref, bd_ref, y_ref, *, tm, chunks):
    r = pl.program_id(0)

    @pl.when(r < nu_ref[0])
    def _():
        y = jnp.dot(h_ref[...], wd_ref[0], preferred_element_type=F32) + bd_ref[0]
        _write_chunks(y_ref, y, tm, chunks)

    @pl.when(r >= nu_ref[0])
    def _():
        y_ref[...] = jnp.zeros_like(y_ref)


def _moe_down(block_e, n_used, hid, w_d, b_d, tm):
    n_rows, f = hid.shape
    d = w_d.shape[2]
    chunks = d // LANES
    grid_spec = pltpu.PrefetchScalarGridSpec(
        num_scalar_prefetch=2,
        grid=(n_rows // tm,),
        in_specs=[pl.BlockSpec((tm, f), lambda r, be, nu: (r, 0)),
                  pl.BlockSpec((1, f, d), lambda r, be, nu: (be[r], 0, 0)),
                  pl.BlockSpec((1, 1, d), lambda r, be, nu: (be[r], 0, 0))],
        out_specs=pl.BlockSpec((tm * chunks, LANES), lambda r, be, nu: (r, 0)))
    return pl.pallas_call(
        functools.partial(_moe_down_kernel, tm=tm, chunks=chunks),
        grid_spec=grid_spec,
        out_shape=jax.ShapeDtypeStruct((n_rows * chunks, LANES), F32),
        compiler_params=_cparams(("arbitrary",)),
        name="moe_down",
    )(block_e, n_used, hid, w_d, b_d)


def _combine_kernel(pos_ref, ys_hbm, h_ref, gw_ref, ple_ref, wpg_ref, wpp_ref, gple_ref, gfin_ref,
                    y_ref, buf, sem, *, tm, chunks, pitch):
    i = pl.program_id(0)

    def copy(s):
        return _row_copy(ys_hbm, pos_ref[i * tm * TOP_K + s], buf.at[s % TOP_K], s // TOP_K,
                         chunks, pitch, sem)

    def issue(s, c):
        copy(s).start()
        return c

    def drain(s, c):
        copy(s).wait()
        return c

    lax.fori_loop(0, tm * TOP_K, issue, 0)
    lax.fori_loop(0, tm * TOP_K, drain, 0)

    gw = gw_ref[...]
    h = h_ref[...]
    for k in range(TOP_K):
        h = h + gw[:, k:k + 1] * _read_chunks(buf, (k,), tm, chunks, pitch)
    u = _rms(h, gple_ref[...]).astype(BF16)
    gate = jax.nn.sigmoid(jnp.dot(u, wpg_ref[...], preferred_element_type=F32))
    proj = jnp.dot(ple_ref[...].astype(BF16), wpp_ref[...], preferred_element_type=F32)
    y_ref[...] = _rms(h + gate * proj, gfin_ref[...])


def _combine(pos, ys3, h, gw, ple, w_pg, w_pp, g_ple, g_fin, tm):
    n, d = h.shape
    chunks = d // LANES
    pitch = _pitch(chunks)
    row = lambda w: pl.BlockSpec((tm, w), lambda i, p: (i, 0))
    const = lambda a: _const_spec(a.shape)
    grid_spec = pltpu.PrefetchScalarGridSpec(
        num_scalar_prefetch=1,
        grid=(n // tm,),
        in_specs=[pl.BlockSpec(memory_space=pl.ANY), row(d), row(LANES), row(ple.shape[1]),
                  const(w_pg), const(w_pp), const(g_ple), const(g_fin)],
        out_specs=row(d),
        scratch_shapes=[pltpu.VMEM((TOP_K, tm * pitch, LANES), F32),
                        pltpu.SemaphoreType.DMA(())])
    return pl.pallas_call(
        functools.partial(_combine_kernel, tm=tm, chunks=chunks, pitch=pitch),
        grid_spec=grid_spec,
        out_shape=jax.ShapeDtypeStruct((n, d), F32),
        compiler_params=_cparams(("arbitrary",)),
        name="moe_combine",
    )(pos, ys3, h, gw, ple, w_pg, w_pp, g_ple, g_fin)


def _route(top_idx, n_experts, tm):
    n_tok = top_idx.shape[0]
    n_slot = n_tok * TOP_K
    n_rows = -(-(n_slot + n_experts * (tm - 1)) // tm) * tm
    flat_e = top_idx.reshape(-1)
    onehot = (flat_e[:, None] == jnp.arange(n_experts, dtype=jnp.int32)[None, :]).astype(jnp.int32)
    csum = jnp.cumsum(onehot, axis=0)
    rank = jnp.sum(onehot * csum, axis=1) - 1
    counts = csum[-1]
    padded = (counts + tm - 1) // tm * tm
    pad_ends = jnp.cumsum(padded)
    pad_starts = pad_ends - padded
    dest = (jnp.sum(onehot * pad_starts[None, :], axis=1) + rank).astype(jnp.int32)
    flat_tok = jnp.repeat(jnp.arange(n_tok, dtype=jnp.int32), TOP_K)
    row_tok = jnp.zeros((n_rows,), jnp.int32).at[dest].set(flat_tok)
    block_start = jnp.arange(0, n_rows, tm, dtype=jnp.int32)
    block_e = jnp.minimum(jnp.sum(pad_ends[None, :] <= block_start[:, None], axis=1),
                          n_experts - 1).astype(jnp.int32)
    n_used = (pad_ends[-1:] // tm).astype(jnp.int32)
    return dest, row_tok, block_e, n_used, n_rows


def _pick(n, prefs):
    for p in prefs:
        if n % p == 0:
            return p
    return n


def _trunk(x, ple, attend, W, lam_init):
    n, d = x.shape
    tm = _pick(n, (512, 256, 128))
    q_sb, kv_sb, q_df, kv_df, gate = _norm_proj(x, W['norm_mix'], W['w_in'], tm)
    o_sb, o_df = attend(q_sb, kv_sb, q_df, kv_df)

    tm = _pick(n, (256, 128))
    n_experts = W['w_router'].shape[1]
    h1, u3, top_idx, gw = _post_attn(o_sb, o_df, gate, x, W['w_br_sb'], W['w_br_df'], W['w_out'],
                                     W['norm_moe'], W['w_router'], W['b_router'], tm)

    n_slot = n * TOP_K
    avg = max(1, n_slot // n_experts)
    tm_moe = max(16, min(256, 1 << (avg.bit_length() - 1)))
    dest, row_tok, block_e, n_used, n_rows = _route(top_idx[:, :TOP_K], n_experts, tm_moe)
    xg = _moe_gather(row_tok, n_used, u3, d // LANES, n_rows, tm_moe)
    hid = _moe_up(block_e, n_used, xg, W['w_gate_up'], W['b_gate_up'], tm_moe)
    ys3 = _moe_down(block_e, n_used, hid, W['w_down'], W['b_down'], tm_moe)
    y = _combine(dest, ys3, h1, gw, ple, W['w_ple_gate'], W['w_ple_proj'], W['norm_ple'],
                 W['norm_final'], _pick(n, (256, 128)))
    return y, kv_sb, kv_df


def kernel(x_prompt, x_sample, cache_sb, cache_df, page_table, p_prompt, p_sample, norm_mix, w_in,
           df_lambda, df_subln, w_br_sb, w_br_df, w_out, norm_moe, w_router, b_router, w_gate_up,
           b_gate_up, w_down, b_down, norm_ple, w_ple_gate, w_ple_proj, norm_final):
    depth = w_in.shape[0]
    assert depth == 1
    b, t, d = x_prompt.shape
    bs, ts, _ = x_sample.shape
    dh = d // 16
    lam_init = _lambda_init(0)
    n_exp = w_router.shape[2]
    W = dict(
        norm_mix=norm_mix[0].reshape(1, d), w_in=w_in[0].astype(BF16),
        w_br_sb=w_br_sb[0].astype(BF16), w_br_df=w_br_df[0].astype(BF16),
        w_out=w_out[0].astype(BF16), norm_moe=norm_moe[0].reshape(1, d),
        w_router=w_router[0], b_router=b_router[0].reshape(1, n_exp),
        w_gate_up=w_gate_up[0].astype(BF16),
        b_gate_up=b_gate_up[0].reshape(n_exp, 1, -1),
        w_down=w_down[0].astype(BF16), b_down=b_down[0].reshape(n_exp, 1, d),
        norm_ple=norm_ple[0].reshape(1, d), w_ple_gate=w_ple_gate[0].astype(BF16),
        w_ple_proj=w_ple_proj[0].astype(BF16), norm_final=norm_final.reshape(1, d))
    slopes = jnp.exp2(-8.0 * jnp.arange(1, DF_HEADS + 1, dtype=F32) / DF_HEADS).reshape(1, DF_HEADS)
    lam_w = df_lambda[0]
    g_sub = df_subln[0].reshape(1, 2 * dh)

    def attend_prompt(q_sb, kv_sb, q_df, kv_df):
        tq = _pick(t, (128,))
        o_sb = _sb_prompt(q_sb.reshape(b, t, -1), kv_sb.reshape(b, t, -1), b, t, tq)
        o_df = _df_prompt(q_df.reshape(b, t, -1), kv_df.reshape(b, t, -1), slopes, lam_w, g_sub,
                          b, t, tq, lam_init)
        return o_sb.reshape(b * t, -1), o_df.reshape(b * t, -1)

    def attend_sample(q_sb, kv_sb, q_df, kv_df):
        return _sample_attn(q_sb, kv_sb, q_df, kv_df, cache_sb[0], cache_df[0], page_table,
                            slopes, lam_w, g_sub, ts, lam_init)

    y_p, sb_p, df_p = _trunk(x_prompt.reshape(b * t, d), p_prompt[0].reshape(b * t, -1),
                             attend_prompt, W, lam_init)
    y_s, sb_s, df_s = _trunk(x_sample.reshape(bs * ts, d), p_sample[0].reshape(bs * ts, -1),
                             attend_sample, W, lam_init)
    return (y_p.reshape(b, t, d), y_s.reshape(bs, ts, d),
            sb_p.reshape(depth, b, t, 2, SB_HEADS, dh),
            df_p.reshape(depth, b, t, 2, DF_HEADS, 2 * dh),
            sb_s.reshape(depth, bs, ts, 2, SB_HEADS, dh),
            df_s.reshape(depth, bs, ts, 2, DF_HEADS, 2 * dh))
```

```python
import functools
import math

import jax
import jax.numpy as jnp
from jax import lax
from jax.experimental import pallas as pl
from jax.experimental.pallas import tpu as pltpu

F32 = jnp.float32
BF16 = jnp.bfloat16

SB_HEADS = 8
DF_HEADS = 4
N_GROUPS = 8
TOP_K = 4
SWIGLU_LIMIT = 7.0
SWIGLU_ALPHA = 1.702
RMS_EPS = 1e-6
LANES = 128
NEG_BIG = -1e30
VMEM_LIMIT = 56 * 1024 * 1024


def _lambda_init(i):
    return 0.8 - 0.6 * math.exp(-0.3 * i)


def _cparams(sem, vmem=VMEM_LIMIT):
    return pltpu.CompilerParams(dimension_semantics=sem, vmem_limit_bytes=vmem)


def _rms(x, g):
    return x * lax.rsqrt(jnp.mean(x * x, axis=-1, keepdims=True) + RMS_EPS) * g


def _const_spec(shape):
    nd = len(shape)
    return pl.BlockSpec(shape, lambda *_: (0,) * nd, pipeline_mode=pl.Buffered(1))


def _pitch(chunks):
    return chunks + 4 if chunks % 8 == 0 else chunks


def _read_chunks(ref, lead, rows, chunks, pitch):
    return jnp.concatenate(
        [ref[lead + (pl.ds(c, rows, stride=pitch), slice(None))] for c in range(chunks)], axis=1)


def _write_chunks(ref, val, rows, chunks):
    for c in range(chunks):
        ref[pl.ds(c, rows, stride=chunks), :] = val[:, c * LANES:(c + 1) * LANES]


def _norm_proj_kernel(x_ref, g_ref, w_ref, qsb_ref, kvsb_ref, qdf_ref, kvdf_ref, gate_ref,
                      u_ref, *, sb_scale, df_scale):
    j = pl.program_id(1)

    @pl.when(j == 0)
    def _():
        u_ref[...] = _rms(x_ref[...], g_ref[...]).astype(BF16)

    acc = jnp.dot(u_ref[...], w_ref[...], preferred_element_type=F32)

    @pl.when(j == 0)
    def _():
        qsb_ref[...] = (acc * sb_scale).astype(BF16)

    @pl.when((j >= 1) & (j < 3))
    def _():
        kvsb_ref[...] = acc

    @pl.when(j == 3)
    def _():
        qdf_ref[...] = (acc * df_scale).astype(BF16)

    @pl.when((j >= 4) & (j < 6))
    def _():
        kvdf_ref[...] = acc

    @pl.when(j >= 6)
    def _():
        gate_ref[...] = jax.nn.sigmoid(acc).astype(BF16)


def _norm_proj(x, g, w, tm):
    n, d = x.shape
    tn = d // 2
    n_in = w.shape[1]
    assert n_in == 10 * tn and n % tm == 0
    dh = d // 16
    kern = functools.partial(_norm_proj_kernel, sb_scale=1.0 / math.sqrt(dh),
                             df_scale=1.0 / math.sqrt(dh))
    clip = lambda j, lo, n_t: jnp.clip(j - lo, 0, n_t - 1)
    return pl.pallas_call(
        kern,
        grid=(n // tm, 10),
        in_specs=[pl.BlockSpec((tm, d), lambda i, j: (i, 0)),
                  pl.BlockSpec((1, d), lambda i, j: (0, 0)),
                  pl.BlockSpec((d, tn), lambda i, j: (0, j))],
        out_specs=[pl.BlockSpec((tm, tn), lambda i, j: (i, 0)),
                   pl.BlockSpec((tm, tn), lambda i, j: (i, clip(j, 1, 2))),
                   pl.BlockSpec((tm, tn), lambda i, j: (i, 0)),
                   pl.BlockSpec((tm, tn), lambda i, j: (i, clip(j, 4, 2))),
                   pl.BlockSpec((tm, tn), lambda i, j: (i, clip(j, 6, 4)))],
        out_shape=[jax.ShapeDtypeStruct((n, tn), BF16),
                   jax.ShapeDtypeStruct((n, 2 * tn), F32),
                   jax.ShapeDtypeStruct((n, tn), BF16),
                   jax.ShapeDtypeStruct((n, 2 * tn), F32),
                   jax.ShapeDtypeStruct((n, 4 * tn), BF16)],
        scratch_shapes=[pltpu.VMEM((tm, d), BF16)],
        compiler_params=_cparams(("parallel", "arbitrary")),
        name="norm_proj",
    )(x, g, w)


def _softplus(z):
    return jnp.maximum(z, 0.0) + jnp.log(1.0 + jnp.exp(-jnp.abs(z)))


def _sb_block(q, k, v, tri2, sub, mask, carry, acc):
    z = lax.dot_general(q, k, (((1,), (1,)), ((), ())), preferred_element_type=F32)
    lf = -_softplus(z)
    if mask is not None:
        lf = jnp.where(mask, lf, 0.0)
    after = []
    for s in reversed(range(k.shape[0] // sub)):
        lfs = lf[:, s * sub:(s + 1) * sub]
        hi = lfs.astype(BF16)
        lo = (lfs - hi.astype(F32)).astype(BF16)
        inner = jnp.dot(jnp.concatenate([hi, lo], axis=1), tri2, preferred_element_type=F32)
        after.append(inner + carry)
        carry = carry + jnp.sum(lfs, axis=-1, keepdims=True)
    after = after[0] if len(after) == 1 else jnp.concatenate(after[::-1], axis=1)
    a = jnp.exp(z + lf + after)
    if mask is not None:
        a = jnp.where(mask, a, 0.0)
    acc = acc + jnp.dot(a.astype(BF16), v, preferred_element_type=F32)
    return carry, acc


def _df_block(q, k, v, bias, mask, m, l, acc):
    s = lax.dot_general(q, k, (((1,), (1,)), ((), ())), preferred_element_type=F32) - bias
    if mask is not None:
        s = jnp.where(mask, s, NEG_BIG)
    m_new = jnp.maximum(m, jnp.max(s, axis=-1, keepdims=True))
    alpha = jnp.exp(m - m_new)
    p = jnp.exp(s - m_new)
    l = alpha * l + jnp.sum(p, axis=-1, keepdims=True)
    acc = alpha * acc + jnp.dot(p.astype(BF16), v, preferred_element_type=F32)
    return m_new, l, acc


def _df_lambda(lam_ref, lam_init):
    lq1, lk1 = lam_ref[0:1, :], lam_ref[1:2, :]
    lq2, lk2 = lam_ref[2:3, :], lam_ref[3:4, :]
    s1 = jnp.sum(lq1 * lk1, axis=-1, keepdims=True)
    s2 = jnp.sum(lq2 * lk2, axis=-1, keepdims=True)
    return jnp.exp(s1) - jnp.exp(s2) + lam_init


def _tri2(sub):
    r = lax.broadcasted_iota(jnp.int32, (2 * sub, sub), 0)
    c = lax.broadcasted_iota(jnp.int32, (2 * sub, sub), 1)
    return ((r > c) & ((r < sub) | (r > c + sub))).astype(BF16)


def _sb_prompt_kernel(q_ref, k_ref, v_ref, o_ref, *, tq):
    i = pl.program_id(2)
    q = q_ref[0]
    dh = q.shape[-1]
    tri2 = _tri2(tq)
    row = lax.broadcasted_iota(jnp.int32, (tq, tq), 0)
    col = lax.broadcasted_iota(jnp.int32, (tq, tq), 1)

    def tile(kb, mask, carry, acc):
        start = pl.multiple_of(kb * tq, tq)
        k = k_ref[0, pl.ds(start, tq), :].astype(BF16)
        v = v_ref[0, pl.ds(start, tq), :].astype(BF16)
        return _sb_block(q, k, v, tri2, tq, mask, carry, acc)

    init = tile(i, col < row, jnp.zeros((tq, 1), F32), jnp.zeros((tq, dh), F32))
    _, acc = lax.fori_loop(0, i, lambda n, c: tile(i - 1 - n, None, *c), init)
    o_ref[0] = acc.astype(o_ref.dtype)


def _sb_prompt(q, kv, b, t, tq):
    dh = q.shape[-1] // SB_HEADS
    return pl.pallas_call(
        functools.partial(_sb_prompt_kernel, tq=tq),
        grid=(b, SB_HEADS, t // tq),
        in_specs=[pl.BlockSpec((1, tq, dh), lambda bb, h, i: (bb, i, h)),
                  pl.BlockSpec((1, t, dh), lambda bb, h, i: (bb, 0, h)),
                  pl.BlockSpec((1, t, dh), lambda bb, h, i: (bb, 0, SB_HEADS + h))],
        out_specs=pl.BlockSpec((1, tq, dh), lambda bb, h, i: (bb, i, h)),
        out_shape=jax.ShapeDtypeStruct(q.shape, BF16),
        compiler_params=_cparams(("parallel", "parallel", "arbitrary")),
        name="sb_prompt",
    )(q, kv, kv)


def _df_finish(o1, o2, lam, g, lam_init):
    o = o1 - lam * o2
    return _rms(o, g) * (1.0 - lam_init)


def _df_prompt_kernel(slope_ref, q_ref, k_ref, v_ref, lam_ref, g_ref, o_ref, *, tq, lam_init):
    h = pl.program_id(1)
    i = pl.program_id(2)
    dh = q_ref.shape[-1] // 2
    q1 = q_ref[0, :, :dh]
    q2 = q_ref[0, :, dh:]
    slope = slope_ref[0, h]
    row = lax.broadcasted_iota(jnp.int32, (tq, tq), 0)
    col = lax.broadcasted_iota(jnp.int32, (tq, tq), 1)
    rel = slope * (row - col).astype(F32)

    def tile(kb, mask, c):
        m1, l1, a1, m2, l2, a2 = c
        start = pl.multiple_of(kb * tq, tq)
        k1 = k_ref[0, pl.ds(start, tq), :dh].astype(BF16)
        k2 = k_ref[0, pl.ds(start, tq), dh:].astype(BF16)
        v = v_ref[0, pl.ds(start, tq), :].astype(BF16)
        bias = rel + slope * ((i - kb) * tq).astype(F32)
        m1, l1, a1 = _df_block(q1, k1, v, bias, mask, m1, l1, a1)
        m2, l2, a2 = _df_block(q2, k2, v, bias, mask, m2, l2, a2)
        return m1, l1, a1, m2, l2, a2

    zc = jnp.zeros((tq, 1), F32)
    za = jnp.zeros((tq, 2 * dh), F32)
    init = tile(i, col <= row, (zc + NEG_BIG, zc, za, zc + NEG_BIG, zc, za))
    _, l1, a1, _, l2, a2 = lax.fori_loop(0, i, lambda kb, c: tile(kb, None, c), init)
    lam = _df_lambda(lam_ref, lam_init)
    o_ref[0] = _df_finish(a1 / l1, a2 / l2, lam, g_ref[...], lam_init).astype(o_ref.dtype)


def _df_prompt(q, kv, slopes, df_lambda, g, b, t, tq, lam_init):
    w = q.shape[-1] // DF_HEADS
    return pl.pallas_call(
        functools.partial(_df_prompt_kernel, tq=tq, lam_init=lam_init),
        grid=(b, DF_HEADS, t // tq),
        in_specs=[pl.BlockSpec(memory_space=pltpu.SMEM),
                  pl.BlockSpec((1, tq, w), lambda bb, h, i: (bb, i, h)),
                  pl.BlockSpec((1, t, w), lambda bb, h, i: (bb, 0, h)),
                  pl.BlockSpec((1, t, w), lambda bb, h, i: (bb, 0, DF_HEADS + h)),
                  _const_spec(df_lambda.shape),
                  _const_spec(g.shape)],
        out_specs=pl.BlockSpec((1, tq, w), lambda bb, h, i: (bb, i, h)),
        out_shape=jax.ShapeDtypeStruct(q.shape, BF16),
        compiler_params=_cparams(("parallel", "parallel", "arbitrary")),
        name="df_prompt",
    )(slopes, q, kv, kv, df_lambda, g)


def _sample_attn_kernel(pt_ref, qsb_ref, qdf_ref, nsb_ref, ndf_ref, *rest,
                        n_pages, page, pps, t_new, lam_init):
    csb_refs, cdf_refs = rest[:pps], rest[pps:2 * pps]
    (slope_ref, lam_ref, g_ref, osb_ref, odf_ref,
     carry_ref, asb_ref, m_ref, l_ref, adf_ref) = rest[2 * pps:]
    j = pl.program_id(1)
    n_rows = qsb_ref.shape[1]
    width = qsb_ref.shape[2]
    past = n_pages * page
    tri2 = _tri2(page)

    @pl.when(j == 0)
    def _():
        carry_ref[...] = jnp.zeros_like(carry_ref)
        asb_ref[...] = jnp.zeros_like(asb_ref)
        m_ref[...] = jnp.full_like(m_ref, NEG_BIG)
        l_ref[...] = jnp.zeros_like(l_ref)
        adf_ref[...] = jnp.zeros_like(adf_ref)

    def step(ksb, vsb, kdf, vdf, kbase, masked):
        tk = ksb.shape[0]
        row = lax.broadcasted_iota(jnp.int32, (n_rows, tk), 0)
        col = lax.broadcasted_iota(jnp.int32, (n_rows, tk), 1)
        qpos = past + row % t_new
        kpos = kbase + col
        carry, asb = _sb_block(qsb_ref[0], ksb, vsb, tri2, page, (kpos < qpos) if masked else None,
                               carry_ref[...], asb_ref[...])
        carry_ref[...] = carry
        asb_ref[...] = asb
        bias = slope_ref[...] * (qpos - kpos).astype(F32)
        m, l, adf = _df_block(qdf_ref[0], kdf, vdf, bias, (kpos <= qpos) if masked else None,
                              m_ref[...], l_ref[...], adf_ref[...])
        m_ref[...] = m
        l_ref[...] = l
        adf_ref[...] = adf

    def groups(ref2d, kv):
        per = 2 * N_GROUPS
        return jnp.concatenate(
            [ref2d[pl.ds(kv * N_GROUPS + g, page, stride=per), :] for g in range(N_GROUPS)],
            axis=1).astype(BF16)

    @pl.when(j == 0)
    def _():
        pad = jnp.zeros((page - t_new, width), F32)
        nsb = nsb_ref[...]
        ndf = ndf_ref[...]
        ksb = jnp.concatenate([nsb[:, :width], pad], axis=0).astype(BF16)
        vsb = jnp.concatenate([nsb[:, width:], pad], axis=0).astype(BF16)
        kdf = jnp.concatenate([ndf[:, :width], pad], axis=0).astype(BF16)
        vdf = jnp.concatenate([ndf[:, width:], pad], axis=0).astype(BF16)
        step(ksb, vsb, kdf, vdf, past, True)

    @pl.when(j > 0)
    def _():
        cat = lambda refs, kv: jnp.concatenate([groups(r.at[0], kv) for r in refs], axis=0)
        step(cat(csb_refs, 0), cat(csb_refs, 1), cat(cdf_refs, 0), cat(cdf_refs, 1),
             (n_pages - j * pps) * page, False)

    @pl.when(j == n_pages // pps)
    def _():
        dh = width // N_GROUPS
        asb = asb_ref[...]
        osb_ref[...] = jnp.concatenate(
            [asb[h * t_new:(h + 1) * t_new, h * dh:(h + 1) * dh] for h in range(SB_HEADS)], axis=1)
        adf = adf_ref[...] / l_ref[...]
        lam = _df_lambda(lam_ref, lam_init)
        outs = []
        for h in range(DF_HEADS):
            def pick(m, h=h):
                rows = adf[(m * DF_HEADS + h) * t_new:(m * DF_HEADS + h + 1) * t_new, :]
                return jnp.concatenate([rows[:, h * dh:(h + 1) * dh],
                                        rows[:, (DF_HEADS + h) * dh:(DF_HEADS + h + 1) * dh]],
                                       axis=1)
            outs.append(_df_finish(pick(0), pick(1), lam, g_ref[...], lam_init))
        odf_ref[...] = jnp.concatenate(outs, axis=1)


def _block_diag_q(q, t_new):
    n, width = q.shape
    b = n // t_new
    q3 = q.reshape(b, 1, t_new, width)
    grp_r = jnp.arange(N_GROUPS)[:, None, None]
    grp_c = (jnp.arange(width) // (width // N_GROUPS))[None, None, :]
    out = jnp.where(grp_r == grp_c, q3, jnp.zeros((), q.dtype))
    return out.reshape(b, N_GROUPS * t_new, width)


def _sample_attn(q_sb, kv_sb, q_df, kv_df, cache_sb, cache_df, page_table, slopes, df_lambda, g,
                 t_new, lam_init):
    n, width = q_sb.shape
    b = n // t_new
    dh = width // N_GROUPS
    n_pages = page_table.shape[1]
    n_pool, page = cache_sb.shape[0], cache_sb.shape[1]
    csb = cache_sb.reshape(n_pool, page * 2 * N_GROUPS, dh)
    cdf = cache_df.reshape(n_pool, page, 2, DF_HEADS, 2, dh).transpose(0, 1, 2, 4, 3, 5)
    cdf = cdf.reshape(n_pool, page * 2 * N_GROUPS, dh)
    half_major = lambda a, lead: a.reshape(lead + (DF_HEADS, 2, dh)).swapaxes(-2, -3).reshape(
        lead + (N_GROUPS * dh,))
    q_df = half_major(q_df, (n,))
    kv_df = half_major(kv_df, (n, 2)).reshape(n, 2 * width)
    n_rows = N_GROUPS * t_new
    slope_rows = jnp.tile(jnp.repeat(slopes.reshape(-1), t_new), 2).reshape(n_rows, 1)
    pt = page_table.reshape(-1).astype(jnp.int32)

    pps = _pick(n_pages, (4, 2))
    if pps == n_pages and n_pages > 4:
        pps = 1

    def page_idx(s):
        return lambda bb, j, pt_ref: (
            pt_ref[bb * n_pages + n_pages - jnp.maximum(j, 1) * pps + s], 0, 0)

    kern = functools.partial(_sample_attn_kernel, n_pages=n_pages, page=page, pps=pps,
                             t_new=t_new, lam_init=lam_init)
    grid_spec = pltpu.PrefetchScalarGridSpec(
        num_scalar_prefetch=1,
        grid=(b, n_pages // pps + 1),
        in_specs=[pl.BlockSpec((1, n_rows, width), lambda bb, j, p: (bb, 0, 0)),
                  pl.BlockSpec((1, n_rows, width), lambda bb, j, p: (bb, 0, 0)),
                  pl.BlockSpec((t_new, 2 * width), lambda bb, j, p: (bb, 0)),
                  pl.BlockSpec((t_new, 2 * width), lambda bb, j, p: (bb, 0))]
                 + [pl.BlockSpec((1,) + csb.shape[1:], page_idx(s)) for s in range(pps)]
                 + [pl.BlockSpec((1,) + cdf.shape[1:], page_idx(s)) for s in range(pps)]
                 + [pl.BlockSpec((n_rows, 1), lambda bb, j, p: (0, 0)),
                  pl.BlockSpec(df_lambda.shape, lambda bb, j, p: (0, 0)),
                  pl.BlockSpec(g.shape, lambda bb, j, p: (0, 0))],
        out_specs=[pl.BlockSpec((t_new, width), lambda bb, j, p: (bb, 0)),
                   pl.BlockSpec((t_new, width), lambda bb, j, p: (bb, 0))],
        scratch_shapes=[pltpu.VMEM((n_rows, 1), F32), pltpu.VMEM((n_rows, width), F32),
                        pltpu.VMEM((n_rows, 1), F32), pltpu.VMEM((n_rows, 1), F32),
                        pltpu.VMEM((n_rows, width), F32)])
    return pl.pallas_call(
        kern,
        grid_spec=grid_spec,
        out_shape=[jax.ShapeDtypeStruct((n, width), F32), jax.ShapeDtypeStruct((n, width), F32)],
        compiler_params=_cparams(("parallel", "arbitrary")),
        name="sample_attn",
    )(pt, _block_diag_q(q_sb, t_new), _block_diag_q(q_df, t_new), kv_sb, kv_df,
      *([csb] * pps), *([cdf] * pps), slope_rows, df_lambda, g)


def _post_attn_kernel(osb_ref, odf_ref, gate_ref, x_ref, wsb_ref, wdf_ref, wout_ref, gmoe_ref,
                      wr_ref, br_ref, h_ref, u_ref, idx_ref, gw_ref, *, n_experts):
    d = x_ref.shape[-1]
    br_sb = jnp.dot(osb_ref[...].astype(BF16), wsb_ref[...], preferred_element_type=F32)
    br_df = jnp.dot(odf_ref[...].astype(BF16), wdf_ref[...], preferred_element_type=F32)
    gate = gate_ref[...].astype(F32)
    merged = gate[:, :d] * br_sb + gate[:, d:] * br_df
    h = x_ref[...] + jnp.dot(merged.astype(BF16), wout_ref[...], preferred_element_type=F32)
    h_ref[...] = h
    u = _rms(h, gmoe_ref[...])
    _write_chunks(u_ref, u, u.shape[0], d // LANES)
    logits = jnp.dot(u, wr_ref[...], preferred_element_type=F32,
                     precision=lax.Precision.HIGHEST) + br_ref[...]
    tm = logits.shape[0]
    lane = lax.broadcasted_iota(jnp.int32, (tm, n_experts), 1)
    out_lane = lax.broadcasted_iota(jnp.int32, (tm, LANES), 1)
    idx_out = jnp.zeros((tm, LANES), jnp.int32)
    val_out = jnp.zeros((tm, LANES), F32)
    top = None
    denom = jnp.zeros((tm, 1), F32)
    for k in range(TOP_K):
        mx = jnp.max(logits, axis=-1, keepdims=True)
        idx = jnp.min(jnp.where(logits == mx, lane, n_experts), axis=-1, keepdims=True)
        logits = jnp.where(lane == idx, -jnp.inf, logits)
        if top is None:
            top = mx
        e = jnp.exp(mx - top)
        denom = denom + e
        idx_out = jnp.where(out_lane == k, idx, idx_out)
        val_out = jnp.where(out_lane == k, e, val_out)
    idx_ref[...] = idx_out
    gw_ref[...] = val_out / denom


def _post_attn(o_sb, o_df, gate, x, w_sb, w_df, w_out, g_moe, w_router, b_router, tm):
    n, d = x.shape
    e = w_router.shape[1]
    row = lambda w: pl.BlockSpec((tm, w), lambda i: (i, 0))
    return pl.pallas_call(
        functools.partial(_post_attn_kernel, n_experts=e),
        grid=(n // tm,),
        in_specs=[row(o_sb.shape[1]), row(o_df.shape[1]), row(gate.shape[1]), row(d),
                  _const_spec(w_sb.shape), _const_spec(w_df.shape), _const_spec(w_out.shape),
                  _const_spec(g_moe.shape), _const_spec(w_router.shape),
                  _const_spec(b_router.shape)],
        out_specs=[row(d),
                   pl.BlockSpec((tm * (d // LANES), LANES), lambda i: (i, 0)),
                   row(LANES), row(LANES)],
        out_shape=[jax.ShapeDtypeStruct((n, d), F32),
                   jax.ShapeDtypeStruct((n * (d // LANES), LANES), F32),
                   jax.ShapeDtypeStruct((n, LANES), jnp.int32),
                   jax.ShapeDtypeStruct((n, LANES), F32)],
        compiler_params=_cparams(("parallel",)),
        name="post_attn",
    )(o_sb, o_df, gate, x, w_sb, w_df, w_out, g_moe, w_router, b_router)


def _row_copy(src_hbm, src_row, dst, dst_slot, chunks, pitch, sem):
    src = src_hbm.at[pl.ds(pl.multiple_of(src_row * chunks, chunks), chunks), :]
    return pltpu.make_async_copy(src, dst.at[pl.ds(dst_slot * pitch, chunks), :], sem)


def _start_rows(src_hbm, idx_ref, base, n, dst, chunks, pitch, sem, unroll):
    def start(t):
        _row_copy(src_hbm, idx_ref[base + t], dst, t, chunks, pitch, sem).start()
    if unroll:
        for t in range(n):
            start(t)
    else:
        lax.fori_loop(0, n, lambda t, c: (start(t), c)[1], 0)


def _wait_rows(src_hbm, n, dst, chunks, pitch, sem):
    for t in range(n):
        _row_copy(src_hbm, 0, dst, t, chunks, pitch, sem).wait()


def _moe_up_kernel(tok_ref, be_ref, nu_ref, u_hbm, wg_ref, wu_ref, bg_ref, bu_ref, h_ref,
                   buf, xs_ref, sem, *, tm, chunks, pitch, n_blocks):
    r = pl.program_id(0)
    n_used = nu_ref[0]
    slot = r % 2
    rows = functools.partial(_start_rows, u_hbm, tok_ref, n=tm, chunks=chunks, pitch=pitch)
    wait = lambda s: _wait_rows(u_hbm, tm, buf.at[s], chunks, pitch, sem.at[s])

    @pl.when(r == 0)
    def _():
        rows(base=0, dst=buf.at[0], sem=sem.at[0], unroll=False)

    @pl.when(r < n_used)
    def _():
        wait(slot)
        xs_ref[...] = _read_chunks(buf, (slot,), tm, chunks, pitch).astype(BF16)
        rows(base=(r + 1) * tm, dst=buf.at[1 - slot], sem=sem.at[1 - slot], unroll=True)
        x = xs_ref[...]
        gate = jnp.dot(x, wg_ref[0], preferred_element_type=F32) + bg_ref[0]
        up = jnp.dot(x, wu_ref[0], preferred_element_type=F32) + bu_ref[0]
        gate = jnp.minimum(gate, SWIGLU_LIMIT)
        up = jnp.clip(up, -SWIGLU_LIMIT, SWIGLU_LIMIT)
        h_ref[...] = ((up + 1.0) * gate * jax.nn.sigmoid(SWIGLU_ALPHA * gate)).astype(BF16)

        @pl.when(r == n_blocks - 1)
        def _():
            wait(1 - slot)

    @pl.when(r >= n_used)
    def _():
        h_ref[...] = jnp.zeros_like(h_ref)

    @pl.when(r == n_used)
    def _():
        wait(slot)


def _moe_up(row_tok, block_e, n_used, u3, w_gu, b_gu, chunks, n_rows, tm):
    d = chunks * LANES
    f = w_gu.shape[2] // 2
    pitch = _pitch(chunks)
    n_blocks = n_rows // tm
    grid_spec = pltpu.PrefetchScalarGridSpec(
        num_scalar_prefetch=3,
        grid=(n_blocks,),
        in_specs=[pl.BlockSpec(memory_space=pl.ANY),
                  pl.BlockSpec((1, d, f), lambda r, tok, be, nu: (be[r], 0, 0)),
                  pl.BlockSpec((1, d, f), lambda r, tok, be, nu: (be[r], 0, 1)),
                  pl.BlockSpec((1, 1, f), lambda r, tok, be, nu: (be[r], 0, 0)),
                  pl.BlockSpec((1, 1, f), lambda r, tok, be, nu: (be[r], 0, 1))],
        out_specs=pl.BlockSpec((tm, f), lambda r, tok, be, nu: (r, 0)),
        scratch_shapes=[pltpu.VMEM((2, tm * pitch, LANES), F32), pltpu.VMEM((tm, d), BF16),
                        pltpu.SemaphoreType.DMA((2,))])
    return pl.pallas_call(
        functools.partial(_moe_up_kernel, tm=tm, chunks=chunks, pitch=pitch, n_blocks=n_blocks),
        grid_spec=grid_spec,
        out_shape=jax.ShapeDtypeStruct((n_rows, f), BF16),
        compiler_params=_cparams(("arbitrary",)),
        name="moe_up",
    )(row_tok, block_e, n_used, u3, w_gu, w_gu, b_gu, b_gu)


def _moe_down_kernel(be_ref, nu_ref, h_ref, wd_ref, bd_ref, y_ref, *, tm, chunks):
    r = pl.program_id(0)

    @pl.when(r < nu_ref[0])
    def _():
        y = jnp.dot(h_ref[...], wd_ref[0], preferred_element_type=F32) + bd_ref[0]
        _write_chunks(y_ref, y, tm, chunks)

    @pl.when(r >= nu_ref[0])
    def _():
        y_ref[...] = jnp.zeros_like(y_ref)


def _moe_down(block_e, n_used, hid, w_d, b_d, tm):
    n_rows, f = hid.shape
    d = w_d.shape[2]
    chunks = d // LANES
    grid_spec = pltpu.PrefetchScalarGridSpec(
        num_scalar_prefetch=2,
        grid=(n_rows // tm,),
        in_specs=[pl.BlockSpec((tm, f), lambda r, be, nu: (r, 0)),
                  pl.BlockSpec((1, f, d), lambda r, be, nu: (be[r], 0, 0)),
                  pl.BlockSpec((1, 1, d), lambda r, be, nu: (be[r], 0, 0))],
        out_specs=pl.BlockSpec((tm * chunks, LANES), lambda r, be, nu: (r, 0)))
    return pl.pallas_call(
        functools.partial(_moe_down_kernel, tm=tm, chunks=chunks),
        grid_spec=grid_spec,
        out_shape=jax.ShapeDtypeStruct((n_rows * chunks, LANES), F32),
        compiler_params=_cparams(("arbitrary",)),
        name="moe_down",
    )(block_e, n_used, hid, w_d, b_d)


def _combine_kernel(pos_ref, ys_hbm, h_ref, gw_ref, ple_ref, wpg_ref, wpp_ref, gple_ref, gfin_ref,
                    y_ref, buf, hs_ref, sem, *, tm, chunks, pitch):
    i = pl.program_id(0)
    slot = i % 2

    def start(block, s, unroll):
        for k in range(TOP_K):
            _start_rows(ys_hbm, pos_ref, (block * TOP_K + k) * tm, tm, buf.at[s, k], chunks, pitch,
                        sem.at[s], unroll)

    def wait(s):
        for k in range(TOP_K):
            _wait_rows(ys_hbm, tm, buf.at[s, k], chunks, pitch, sem.at[s])

    @pl.when(i == 0)
    def _():
        start(0, 0, False)

    wait(slot)
    gw = gw_ref[...]
    h = h_ref[...]
    for k in range(TOP_K):
        h = h + gw[:, k:k + 1] * _read_chunks(buf, (slot, k), tm, chunks, pitch)
    hs_ref[...] = h
    start(i + 1, 1 - slot, True)
    h = hs_ref[...]
    u = _rms(h, gple_ref[...]).astype(BF16)
    gate = jax.nn.sigmoid(jnp.dot(u, wpg_ref[...], preferred_element_type=F32))
    proj = jnp.dot(ple_ref[...].astype(BF16), wpp_ref[...], preferred_element_type=F32)
    y_ref[...] = _rms(h + gate * proj, gfin_ref[...])

    @pl.when(i == pl.num_programs(0) - 1)
    def _():
        wait(1 - slot)


def _combine(pos, ys3, h, gw, ple, w_pg, w_pp, g_ple, g_fin, tm):
    n, d = h.shape
    chunks = d // LANES
    pitch = _pitch(chunks)
    row = lambda w: pl.BlockSpec((tm, w), lambda i, p: (i, 0))
    const = lambda a: _const_spec(a.shape)
    grid_spec = pltpu.PrefetchScalarGridSpec(
        num_scalar_prefetch=1,
        grid=(n // tm,),
        in_specs=[pl.BlockSpec(memory_space=pl.ANY), row(d), row(LANES), row(ple.shape[1]),
                  const(w_pg), const(w_pp), const(g_ple), const(g_fin)],
        out_specs=row(d),
        scratch_shapes=[pltpu.VMEM((2, TOP_K, tm * pitch, LANES), F32), pltpu.VMEM((tm, d), F32),
                        pltpu.SemaphoreType.DMA((2,))])
    return pl.pallas_call(
        functools.partial(_combine_kernel, tm=tm, chunks=chunks, pitch=pitch),
        grid_spec=grid_spec,
        out_shape=jax.ShapeDtypeStruct((n, d), F32),
        compiler_params=_cparams(("arbitrary",)),
        name="moe_combine",
    )(pos.reshape(-1), ys3, h, gw, ple, w_pg, w_pp, g_ple, g_fin)


def _route(top_idx, n_experts, tm):
    n_tok = top_idx.shape[0]
    n_slot = n_tok * TOP_K
    n_rows = -(-(n_slot + n_experts * (tm - 1)) // tm) * tm
    flat_e = top_idx.reshape(-1)
    onehot = (flat_e[:, None] == jnp.arange(n_experts, dtype=jnp.int32)[None, :]).astype(jnp.int32)
    csum = jnp.cumsum(onehot, axis=0)
    rank = jnp.sum(onehot * csum, axis=1) - 1
    counts = csum[-1]
    padded = (counts + tm - 1) // tm * tm
    pad_ends = jnp.cumsum(padded)
    pad_starts = pad_ends - padded
    dest = (jnp.sum(onehot * pad_starts[None, :], axis=1) + rank).astype(jnp.int32)
    flat_tok = jnp.repeat(jnp.arange(n_tok, dtype=jnp.int32), TOP_K)
    row_tok = jnp.zeros((n_rows + tm,), jnp.int32).at[dest].set(flat_tok)
    block_start = jnp.arange(0, n_rows, tm, dtype=jnp.int32)
    block_e = jnp.minimum(jnp.sum(pad_ends[None, :] <= block_start[:, None], axis=1),
                          n_experts - 1).astype(jnp.int32)
    n_used = (pad_ends[-1:] // tm).astype(jnp.int32)
    return dest, row_tok, block_e, n_used, n_rows


def _pick(n, prefs):
    for p in prefs:
        if n % p == 0:
            return p
    return n


def _trunk(x, ple, attend, W, lam_init):
    n, d = x.shape
    tm = _pick(n, (512, 256, 128))
    q_sb, kv_sb, q_df, kv_df, gate = _norm_proj(x, W['norm_mix'], W['w_in'], tm)
    o_sb, o_df = attend(q_sb, kv_sb, q_df, kv_df)

    tm = _pick(n, (256, 128))
    n_experts = W['w_router'].shape[1]
    h1, u3, top_idx, gw = _post_attn(o_sb, o_df, gate, x, W['w_br_sb'], W['w_br_df'], W['w_out'],
                                     W['norm_moe'], W['w_router'], W['b_router'], tm)

    n_slot = n * TOP_K
    avg = max(1, n_slot // n_experts)
    tm_moe = max(16, min(256, 1 << (avg.bit_length() - 1)))
    dest, row_tok, block_e, n_used, n_rows = _route(top_idx[:, :TOP_K], n_experts, tm_moe)
    hid = _moe_up(row_tok, block_e, n_used, u3, W['w_gate_up'], W['b_gate_up'], d // LANES,
                  n_rows, tm_moe)
    ys3 = _moe_down(block_e, n_used, hid, W['w_down'], W['b_down'], tm_moe)
    tm_c = _pick(n, (128,))
    pos = dest.reshape(n // tm_c, tm_c, TOP_K).swapaxes(1, 2)
    pos = jnp.concatenate([pos, jnp.zeros((1, TOP_K, tm_c), jnp.int32)], axis=0)
    y = _combine(pos, ys3, h1, gw, ple, W['w_ple_gate'], W['w_ple_proj'], W['norm_ple'],
                 W['norm_final'], tm_c)
    return y, kv_sb, kv_df


def kernel(x_prompt, x_sample, cache_sb, cache_df, page_table, p_prompt, p_sample, norm_mix, w_in,
           df_lambda, df_subln, w_br_sb, w_br_df, w_out, norm_moe, w_router, b_router, w_gate_up,
           b_gate_up, w_down, b_down, norm_ple, w_ple_gate, w_ple_proj, norm_final):
    depth = w_in.shape[0]
    assert depth == 1
    b, t, d = x_prompt.shape
    bs, ts, _ = x_sample.shape
    dh = d // 16
    lam_init = _lambda_init(0)
    n_exp = w_router.shape[2]
    W = dict(
        norm_mix=norm_mix[0].reshape(1, d), w_in=w_in[0].astype(BF16),
        w_br_sb=w_br_sb[0].astype(BF16), w_br_df=w_br_df[0].astype(BF16),
        w_out=w_out[0].astype(BF16), norm_moe=norm_moe[0].reshape(1, d),
        w_router=w_router[0], b_router=b_router[0].reshape(1, n_exp),
        w_gate_up=w_gate_up[0].astype(BF16),
        b_gate_up=b_gate_up[0].reshape(n_exp, 1, -1),
        w_down=w_down[0].astype(BF16), b_down=b_down[0].reshape(n_exp, 1, d),
        norm_ple=norm_ple[0].reshape(1, d), w_ple_gate=w_ple_gate[0].astype(BF16),
        w_ple_proj=w_ple_proj[0].astype(BF16), norm_final=norm_final.reshape(1, d))
    slopes = jnp.exp2(-8.0 * jnp.arange(1, DF_HEADS + 1, dtype=F32) / DF_HEADS).reshape(1, DF_HEADS)
    lam_w = df_lambda[0]
    g_sub = df_subln[0].reshape(1, 2 * dh)

    def attend_prompt(q_sb, kv_sb, q_df, kv_df):
        tq = _pick(t, (256, 128))
        o_sb = _sb_prompt(q_sb.reshape(b, t, -1), kv_sb.reshape(b, t, -1), b, t, tq)
        o_df = _df_prompt(q_df.reshape(b, t, -1), kv_df.reshape(b, t, -1), slopes, lam_w, g_sub,
                          b, t, tq, lam_init)
        return o_sb.reshape(b * t, -1), o_df.reshape(b * t, -1)

    def attend_sample(q_sb, kv_sb, q_df, kv_df):
        return _sample_attn(q_sb, kv_sb, q_df, kv_df, cache_sb[0], cache_df[0], page_table,
                            slopes, lam_w, g_sub, ts, lam_init)

    y_p, sb_p, df_p = _trunk(x_prompt.reshape(b * t, d), p_prompt[0].reshape(b * t, -1),
                             attend_prompt, W, lam_init)
    y_s, sb_s, df_s = _trunk(x_sample.reshape(bs * ts, d), p_sample[0].reshape(bs * ts, -1),
                             attend_sample, W, lam_init)
    return (y_p.reshape(b, t, d), y_s.reshape(bs, ts, d),
            sb_p.reshape(depth, b, t, 2, SB_HEADS, dh),
            df_p.reshape(depth, b, t, 2, DF_HEADS, 2 * dh),
            sb_s.reshape(depth, bs, ts, 2, SB_HEADS, dh),
            df_s.reshape(depth, bs, ts, 2, DF_HEADS, 2 * dh))
```

```python
import functools
import math

import jax
import jax.numpy as jnp
from jax import lax
from jax.experimental import pallas as pl
from jax.experimental.pallas import tpu as pltpu

F32 = jnp.float32
BF16 = jnp.bfloat16

SB_HEADS = 8
DF_HEADS = 4
N_GROUPS = 8
TOP_K = 4
SWIGLU_LIMIT = 7.0
SWIGLU_ALPHA = 1.702
RMS_EPS = 1e-6
LANES = 128
NEG_BIG = -1e30
VMEM_LIMIT = 56 * 1024 * 1024


def _lambda_init(i):
    return 0.8 - 0.6 * math.exp(-0.3 * i)


def _cparams(sem, vmem=VMEM_LIMIT):
    return pltpu.CompilerParams(dimension_semantics=sem, vmem_limit_bytes=vmem)


def _rms(x, g):
    return x * lax.rsqrt(jnp.mean(x * x, axis=-1, keepdims=True) + RMS_EPS) * g


def _const_spec(shape):
    nd = len(shape)
    return pl.BlockSpec(shape, lambda *_: (0,) * nd, pipeline_mode=pl.Buffered(1))


def _pitch(chunks):
    return chunks + 4 if chunks % 8 == 0 else chunks


def _read_chunks(ref, lead, rows, chunks, pitch):
    return jnp.concatenate(
        [ref[lead + (pl.ds(c, rows, stride=pitch), slice(None))] for c in range(chunks)], axis=1)


def _write_chunks(ref, val, rows, chunks):
    for c in range(chunks):
        ref[pl.ds(c, rows, stride=chunks), :] = val[:, c * LANES:(c + 1) * LANES]


def _norm_proj_kernel(x_ref, g_ref, w_ref, qsb_ref, kvsb_ref, qdf_ref, kvdf_ref, gate_ref,
                      u_ref, *, sb_scale, df_scale):
    j = pl.program_id(1)

    @pl.when(j == 0)
    def _():
        u_ref[...] = _rms(x_ref[...], g_ref[...]).astype(BF16)

    acc = jnp.dot(u_ref[...], w_ref[...], preferred_element_type=F32)

    @pl.when(j == 0)
    def _():
        qsb_ref[...] = (acc * sb_scale).astype(BF16)

    @pl.when((j >= 1) & (j < 3))
    def _():
        kvsb_ref[...] = acc

    @pl.when(j == 3)
    def _():
        qdf_ref[...] = (acc * df_scale).astype(BF16)

    @pl.when((j >= 4) & (j < 6))
    def _():
        kvdf_ref[...] = acc

    @pl.when(j >= 6)
    def _():
        gate_ref[...] = jax.nn.sigmoid(acc).astype(BF16)


def _norm_proj(x, g, w, tm):
    n, d = x.shape
    tn = d // 2
    n_in = w.shape[1]
    assert n_in == 10 * tn and n % tm == 0
    dh = d // 16
    kern = functools.partial(_norm_proj_kernel, sb_scale=1.0 / math.sqrt(dh),
                             df_scale=1.0 / math.sqrt(dh))
    clip = lambda j, lo, n_t: jnp.clip(j - lo, 0, n_t - 1)
    return pl.pallas_call(
        kern,
        grid=(n // tm, 10),
        in_specs=[pl.BlockSpec((tm, d), lambda i, j: (i, 0)),
                  pl.BlockSpec((1, d), lambda i, j: (0, 0)),
                  pl.BlockSpec((d, tn), lambda i, j: (0, j))],
        out_specs=[pl.BlockSpec((tm, tn), lambda i, j: (i, 0)),
                   pl.BlockSpec((tm, tn), lambda i, j: (i, clip(j, 1, 2))),
                   pl.BlockSpec((tm, tn), lambda i, j: (i, 0)),
                   pl.BlockSpec((tm, tn), lambda i, j: (i, clip(j, 4, 2))),
                   pl.BlockSpec((tm, tn), lambda i, j: (i, clip(j, 6, 4)))],
        out_shape=[jax.ShapeDtypeStruct((n, tn), BF16),
                   jax.ShapeDtypeStruct((n, 2 * tn), F32),
                   jax.ShapeDtypeStruct((n, tn), BF16),
                   jax.ShapeDtypeStruct((n, 2 * tn), F32),
                   jax.ShapeDtypeStruct((n, 4 * tn), BF16)],
        scratch_shapes=[pltpu.VMEM((tm, d), BF16)],
        compiler_params=_cparams(("parallel", "arbitrary")),
        name="norm_proj",
    )(x, g, w)


def _softplus(z):
    return jnp.maximum(z, 0.0) + jnp.log(1.0 + jnp.exp(-jnp.abs(z)))


def _sb_block(q, k, v, tri2, sub, mask, carry, acc):
    z = lax.dot_general(q, k, (((1,), (1,)), ((), ())), preferred_element_type=F32)
    lf = -_softplus(z)
    if mask is not None:
        lf = jnp.where(mask, lf, 0.0)
    after = []
    for s in reversed(range(k.shape[0] // sub)):
        lfs = lf[:, s * sub:(s + 1) * sub]
        hi = lfs.astype(BF16)
        lo = (lfs - hi.astype(F32)).astype(BF16)
        inner = jnp.dot(jnp.concatenate([hi, lo], axis=1), tri2, preferred_element_type=F32)
        after.append(inner + carry)
        carry = carry + jnp.sum(lfs, axis=-1, keepdims=True)
    after = after[0] if len(after) == 1 else jnp.concatenate(after[::-1], axis=1)
    a = jnp.exp(z + lf + after)
    if mask is not None:
        a = jnp.where(mask, a, 0.0)
    acc = acc + jnp.dot(a.astype(BF16), v, preferred_element_type=F32)
    return carry, acc


def _df_block(q, k, v, bias, mask, m, l, acc):
    s = lax.dot_general(q, k, (((1,), (1,)), ((), ())), preferred_element_type=F32) - bias
    if mask is not None:
        s = jnp.where(mask, s, NEG_BIG)
    m_new = jnp.maximum(m, jnp.max(s, axis=-1, keepdims=True))
    alpha = jnp.exp(m - m_new)
    p = jnp.exp(s - m_new)
    l = alpha * l + jnp.sum(p, axis=-1, keepdims=True)
    acc = alpha * acc + jnp.dot(p.astype(BF16), v, preferred_element_type=F32)
    return m_new, l, acc


def _df_lambda(lam_ref, lam_init):
    lq1, lk1 = lam_ref[0:1, :], lam_ref[1:2, :]
    lq2, lk2 = lam_ref[2:3, :], lam_ref[3:4, :]
    s1 = jnp.sum(lq1 * lk1, axis=-1, keepdims=True)
    s2 = jnp.sum(lq2 * lk2, axis=-1, keepdims=True)
    return jnp.exp(s1) - jnp.exp(s2) + lam_init


def _tri2(sub):
    r = lax.broadcasted_iota(jnp.int32, (2 * sub, sub), 0)
    c = lax.broadcasted_iota(jnp.int32, (2 * sub, sub), 1)
    return ((r > c) & ((r < sub) | (r > c + sub))).astype(BF16)


def _sb_prompt_kernel(q_ref, k_ref, v_ref, o_ref, *, tq):
    i = pl.program_id(2)
    hps = SB_HEADS_PER_STEP
    dh = q_ref.shape[-1] // hps
    head = lambda a, h: a[:, h * dh:(h + 1) * dh]
    q = q_ref[0]
    tri2 = _tri2(tq)
    row = lax.broadcasted_iota(jnp.int32, (tq, tq), 0)
    col = lax.broadcasted_iota(jnp.int32, (tq, tq), 1)

    def tile(kb, mask, state):
        start = pl.multiple_of(kb * tq, tq)
        k = k_ref[0, pl.ds(start, tq), :].astype(BF16)
        v = v_ref[0, pl.ds(start, tq), :].astype(BF16)
        return tuple(_sb_block(head(q, h), head(k, h), head(v, h), tri2, tq, mask, *state[h])
                     for h in range(hps))

    zero = (jnp.zeros((tq, 1), F32), jnp.zeros((tq, dh), F32))
    init = tile(i, col < row, (zero,) * hps)
    state = lax.fori_loop(0, i, lambda n, c: tile(i - 1 - n, None, c), init)
    o_ref[0] = jnp.concatenate([acc for _, acc in state], axis=1).astype(o_ref.dtype)


SB_HEADS_PER_STEP = 2


def _sb_prompt(q, kv, b, t, tq):
    hps = SB_HEADS_PER_STEP
    dh = hps * (q.shape[-1] // SB_HEADS)
    return pl.pallas_call(
        functools.partial(_sb_prompt_kernel, tq=tq),
        grid=(b, SB_HEADS // hps, t // tq),
        in_specs=[pl.BlockSpec((1, tq, dh), lambda bb, h, i: (bb, i, h)),
                  pl.BlockSpec((1, t, dh), lambda bb, h, i: (bb, 0, h)),
                  pl.BlockSpec((1, t, dh), lambda bb, h, i: (bb, 0, SB_HEADS // hps + h))],
        out_specs=pl.BlockSpec((1, tq, dh), lambda bb, h, i: (bb, i, h)),
        out_shape=jax.ShapeDtypeStruct(q.shape, BF16),
        compiler_params=_cparams(("parallel", "parallel", "arbitrary")),
        name="sb_prompt",
    )(q, kv, kv)


def _df_finish(o1, o2, lam, g, lam_init):
    o = o1 - lam * o2
    return _rms(o, g) * (1.0 - lam_init)


def _df_prompt_kernel(slope_ref, q_ref, k_ref, v_ref, lam_ref, g_ref, o_ref, *, tq, lam_init):
    h = pl.program_id(1)
    i = pl.program_id(2)
    dh = q_ref.shape[-1] // 2
    q1 = q_ref[0, :, :dh]
    q2 = q_ref[0, :, dh:]
    slope = slope_ref[0, h]
    row = lax.broadcasted_iota(jnp.int32, (tq, tq), 0)
    col = lax.broadcasted_iota(jnp.int32, (tq, tq), 1)
    rel = slope * (row - col).astype(F32)

    def tile(kb, mask, c):
        m1, l1, a1, m2, l2, a2 = c
        start = pl.multiple_of(kb * tq, tq)
        k1 = k_ref[0, pl.ds(start, tq), :dh].astype(BF16)
        k2 = k_ref[0, pl.ds(start, tq), dh:].astype(BF16)
        v = v_ref[0, pl.ds(start, tq), :].astype(BF16)
        bias = rel + slope * ((i - kb) * tq).astype(F32)
        m1, l1, a1 = _df_block(q1, k1, v, bias, mask, m1, l1, a1)
        m2, l2, a2 = _df_block(q2, k2, v, bias, mask, m2, l2, a2)
        return m1, l1, a1, m2, l2, a2

    zc = jnp.zeros((tq, 1), F32)
    za = jnp.zeros((tq, 2 * dh), F32)
    init = tile(i, col <= row, (zc + NEG_BIG, zc, za, zc + NEG_BIG, zc, za))
    _, l1, a1, _, l2, a2 = lax.fori_loop(0, i, lambda kb, c: tile(kb, None, c), init)
    lam = _df_lambda(lam_ref, lam_init)
    o_ref[0] = _df_finish(a1 / l1, a2 / l2, lam, g_ref[...], lam_init).astype(o_ref.dtype)


def _df_prompt(q, kv, slopes, df_lambda, g, b, t, tq, lam_init):
    w = q.shape[-1] // DF_HEADS
    return pl.pallas_call(
        functools.partial(_df_prompt_kernel, tq=tq, lam_init=lam_init),
        grid=(b, DF_HEADS, t // tq),
        in_specs=[pl.BlockSpec(memory_space=pltpu.SMEM),
                  pl.BlockSpec((1, tq, w), lambda bb, h, i: (bb, i, h)),
                  pl.BlockSpec((1, t, w), lambda bb, h, i: (bb, 0, h)),
                  pl.BlockSpec((1, t, w), lambda bb, h, i: (bb, 0, DF_HEADS + h)),
                  _const_spec(df_lambda.shape),
                  _const_spec(g.shape)],
        out_specs=pl.BlockSpec((1, tq, w), lambda bb, h, i: (bb, i, h)),
        out_shape=jax.ShapeDtypeStruct(q.shape, BF16),
        compiler_params=_cparams(("parallel", "parallel", "arbitrary")),
        name="df_prompt",
    )(slopes, q, kv, kv, df_lambda, g)


def _sample_attn_kernel(pt_ref, qsb_ref, qdf_ref, nsb_ref, ndf_ref, *rest,
                        n_pages, page, pps, t_new, lam_init):
    csb_refs, cdf_refs = rest[:pps], rest[pps:2 * pps]
    (slope_ref, lam_ref, g_ref, osb_ref, odf_ref,
     carry_ref, asb_ref, m_ref, l_ref, adf_ref) = rest[2 * pps:]
    j = pl.program_id(1)
    n_rows = qsb_ref.shape[1]
    width = qsb_ref.shape[2]
    past = n_pages * page
    tri2 = _tri2(page)

    @pl.when(j == 0)
    def _():
        carry_ref[...] = jnp.zeros_like(carry_ref)
        asb_ref[...] = jnp.zeros_like(asb_ref)
        m_ref[...] = jnp.full_like(m_ref, NEG_BIG)
        l_ref[...] = jnp.zeros_like(l_ref)
        adf_ref[...] = jnp.zeros_like(adf_ref)

    def step(ksb, vsb, kdf, vdf, kbase, masked):
        tk = ksb.shape[0]
        row = lax.broadcasted_iota(jnp.int32, (n_rows, tk), 0)
        col = lax.broadcasted_iota(jnp.int32, (n_rows, tk), 1)
        qpos = past + row % t_new
        kpos = kbase + col
        carry, asb = _sb_block(qsb_ref[0], ksb, vsb, tri2, page, (kpos < qpos) if masked else None,
                               carry_ref[...], asb_ref[...])
        carry_ref[...] = carry
        asb_ref[...] = asb
        bias = slope_ref[...] * (qpos - kpos).astype(F32)
        m, l, adf = _df_block(qdf_ref[0], kdf, vdf, bias, (kpos <= qpos) if masked else None,
                              m_ref[...], l_ref[...], adf_ref[...])
        m_ref[...] = m
        l_ref[...] = l
        adf_ref[...] = adf

    def groups(ref2d, kv):
        per = 2 * N_GROUPS
        return jnp.concatenate(
            [ref2d[pl.ds(kv * N_GROUPS + g, page, stride=per), :] for g in range(N_GROUPS)],
            axis=1).astype(BF16)

    @pl.when(j == 0)
    def _():
        pad = jnp.zeros((page - t_new, width), F32)
        nsb = nsb_ref[...]
        ndf = ndf_ref[...]
        ksb = jnp.concatenate([nsb[:, :width], pad], axis=0).astype(BF16)
        vsb = jnp.concatenate([nsb[:, width:], pad], axis=0).astype(BF16)
        kdf = jnp.concatenate([ndf[:, :width], pad], axis=0).astype(BF16)
        vdf = jnp.concatenate([ndf[:, width:], pad], axis=0).astype(BF16)
        step(ksb, vsb, kdf, vdf, past, True)

    @pl.when(j > 0)
    def _():
        cat = lambda refs, kv: jnp.concatenate([groups(r.at[0], kv) for r in refs], axis=0)
        step(cat(csb_refs, 0), cat(csb_refs, 1), cat(cdf_refs, 0), cat(cdf_refs, 1),
             (n_pages - j * pps) * page, False)

    @pl.when(j == n_pages // pps)
    def _():
        dh = width // N_GROUPS
        asb = asb_ref[...]
        osb_ref[...] = jnp.concatenate(
            [asb[h * t_new:(h + 1) * t_new, h * dh:(h + 1) * dh] for h in range(SB_HEADS)], axis=1)
        adf = adf_ref[...] / l_ref[...]
        lam = _df_lambda(lam_ref, lam_init)
        outs = []
        for h in range(DF_HEADS):
            def pick(m, h=h):
                rows = adf[(m * DF_HEADS + h) * t_new:(m * DF_HEADS + h + 1) * t_new, :]
                return jnp.concatenate([rows[:, h * dh:(h + 1) * dh],
                                        rows[:, (DF_HEADS + h) * dh:(DF_HEADS + h + 1) * dh]],
                                       axis=1)
            outs.append(_df_finish(pick(0), pick(1), lam, g_ref[...], lam_init))
        odf_ref[...] = jnp.concatenate(outs, axis=1)


def _block_diag_q(q, t_new):
    n, width = q.shape
    b = n // t_new
    q3 = q.reshape(b, 1, t_new, width)
    grp_r = jnp.arange(N_GROUPS)[:, None, None]
    grp_c = (jnp.arange(width) // (width // N_GROUPS))[None, None, :]
    out = jnp.where(grp_r == grp_c, q3, jnp.zeros((), q.dtype))
    return out.reshape(b, N_GROUPS * t_new, width)


def _sample_attn(q_sb, kv_sb, q_df, kv_df, cache_sb, cache_df, page_table, slopes, df_lambda, g,
                 t_new, lam_init):
    n, width = q_sb.shape
    b = n // t_new
    dh = width // N_GROUPS
    n_pages = page_table.shape[1]
    n_pool, page = cache_sb.shape[0], cache_sb.shape[1]
    csb = cache_sb.reshape(n_pool, page * 2 * N_GROUPS, dh)
    cdf = cache_df.reshape(n_pool, page, 2, DF_HEADS, 2, dh).transpose(0, 1, 2, 4, 3, 5)
    cdf = cdf.reshape(n_pool, page * 2 * N_GROUPS, dh)
    half_major = lambda a, lead: a.reshape(lead + (DF_HEADS, 2, dh)).swapaxes(-2, -3).reshape(
        lead + (N_GROUPS * dh,))
    q_df = half_major(q_df, (n,))
    kv_df = half_major(kv_df, (n, 2)).reshape(n, 2 * width)
    n_rows = N_GROUPS * t_new
    slope_rows = jnp.tile(jnp.repeat(slopes.reshape(-1), t_new), 2).reshape(n_rows, 1)
    pt = page_table.reshape(-1).astype(jnp.int32)

    pps = _pick(n_pages, (4, 2))
    if pps == n_pages and n_pages > 4:
        pps = 1

    def page_idx(s):
        return lambda bb, j, pt_ref: (
            pt_ref[bb * n_pages + n_pages - jnp.maximum(j, 1) * pps + s], 0, 0)

    kern = functools.partial(_sample_attn_kernel, n_pages=n_pages, page=page, pps=pps,
                             t_new=t_new, lam_init=lam_init)
    grid_spec = pltpu.PrefetchScalarGridSpec(
        num_scalar_prefetch=1,
        grid=(b, n_pages // pps + 1),
        in_specs=[pl.BlockSpec((1, n_rows, width), lambda bb, j, p: (bb, 0, 0)),
                  pl.BlockSpec((1, n_rows, width), lambda bb, j, p: (bb, 0, 0)),
                  pl.BlockSpec((t_new, 2 * width), lambda bb, j, p: (bb, 0)),
                  pl.BlockSpec((t_new, 2 * width), lambda bb, j, p: (bb, 0))]
                 + [pl.BlockSpec((1,) + csb.shape[1:], page_idx(s)) for s in range(pps)]
                 + [pl.BlockSpec((1,) + cdf.shape[1:], page_idx(s)) for s in range(pps)]
                 + [pl.BlockSpec((n_rows, 1), lambda bb, j, p: (0, 0)),
                  pl.BlockSpec(df_lambda.shape, lambda bb, j, p: (0, 0)),
                  pl.BlockSpec(g.shape, lambda bb, j, p: (0, 0))],
        out_specs=[pl.BlockSpec((t_new, width), lambda bb, j, p: (bb, 0)),
                   pl.BlockSpec((t_new, width), lambda bb, j, p: (bb, 0))],
        scratch_shapes=[pltpu.VMEM((n_rows, 1), F32), pltpu.VMEM((n_rows, width), F32),
                        pltpu.VMEM((n_rows, 1), F32), pltpu.VMEM((n_rows, 1), F32),
                        pltpu.VMEM((n_rows, width), F32)])
    return pl.pallas_call(
        kern,
        grid_spec=grid_spec,
        out_shape=[jax.ShapeDtypeStruct((n, width), F32), jax.ShapeDtypeStruct((n, width), F32)],
        compiler_params=_cparams(("parallel", "arbitrary")),
        name="sample_attn",
    )(pt, _block_diag_q(q_sb, t_new), _block_diag_q(q_df, t_new), kv_sb, kv_df,
      *([csb] * pps), *([cdf] * pps), slope_rows, df_lambda, g)


def _post_attn_kernel(osb_ref, odf_ref, gate_ref, x_ref, wsb_ref, wdf_ref, wout_ref, gmoe_ref,
                      wr_ref, br_ref, h_ref, u_ref, idx_ref, gw_ref, *, n_experts):
    d = x_ref.shape[-1]
    br_sb = jnp.dot(osb_ref[...].astype(BF16), wsb_ref[...], preferred_element_type=F32)
    br_df = jnp.dot(odf_ref[...].astype(BF16), wdf_ref[...], preferred_element_type=F32)
    gate = gate_ref[...].astype(F32)
    merged = gate[:, :d] * br_sb + gate[:, d:] * br_df
    h = x_ref[...] + jnp.dot(merged.astype(BF16), wout_ref[...], preferred_element_type=F32)
    h_ref[...] = h
    u = _rms(h, gmoe_ref[...])
    _write_chunks(u_ref, u, u.shape[0], d // LANES)
    logits = jnp.dot(u, wr_ref[...], preferred_element_type=F32,
                     precision=lax.Precision.HIGHEST) + br_ref[...]
    tm = logits.shape[0]
    lane = lax.broadcasted_iota(jnp.int32, (tm, n_experts), 1)
    out_lane = lax.broadcasted_iota(jnp.int32, (tm, LANES), 1)
    idx_out = jnp.zeros((tm, LANES), jnp.int32)
    val_out = jnp.zeros((tm, LANES), F32)
    top = None
    denom = jnp.zeros((tm, 1), F32)
    for k in range(TOP_K):
        mx = jnp.max(logits, axis=-1, keepdims=True)
        idx = jnp.min(jnp.where(logits == mx, lane, n_experts), axis=-1, keepdims=True)
        logits = jnp.where(lane == idx, -jnp.inf, logits)
        if top is None:
            top = mx
        e = jnp.exp(mx - top)
        denom = denom + e
        idx_out = jnp.where(out_lane == k, idx, idx_out)
        val_out = jnp.where(out_lane == k, e, val_out)
    idx_ref[...] = idx_out
    gw_ref[...] = val_out / denom


def _post_attn(o_sb, o_df, gate, x, w_sb, w_df, w_out, g_moe, w_router, b_router, tm):
    n, d = x.shape
    e = w_router.shape[1]
    row = lambda w: pl.BlockSpec((tm, w), lambda i: (i, 0))
    return pl.pallas_call(
        functools.partial(_post_attn_kernel, n_experts=e),
        grid=(n // tm,),
        in_specs=[row(o_sb.shape[1]), row(o_df.shape[1]), row(gate.shape[1]), row(d),
                  _const_spec(w_sb.shape), _const_spec(w_df.shape), _const_spec(w_out.shape),
                  _const_spec(g_moe.shape), _const_spec(w_router.shape),
                  _const_spec(b_router.shape)],
        out_specs=[row(d),
                   pl.BlockSpec((tm * (d // LANES), LANES), lambda i: (i, 0)),
                   row(LANES), row(LANES)],
        out_shape=[jax.ShapeDtypeStruct((n, d), F32),
                   jax.ShapeDtypeStruct((n * (d // LANES), LANES), F32),
                   jax.ShapeDtypeStruct((n, LANES), jnp.int32),
                   jax.ShapeDtypeStruct((n, LANES), F32)],
        compiler_params=_cparams(("parallel",)),
        name="post_attn",
    )(o_sb, o_df, gate, x, w_sb, w_df, w_out, g_moe, w_router, b_router)


def _row_copy(src_hbm, src_row, dst, dst_slot, chunks, pitch, sem):
    src = src_hbm.at[pl.ds(pl.multiple_of(src_row * chunks, chunks), chunks), :]
    return pltpu.make_async_copy(src, dst.at[pl.ds(dst_slot * pitch, chunks), :], sem)


def _start_rows(src_hbm, idx_ref, base, n, dst, chunks, pitch, sem, unroll, alternate=False):
    def start(t, priority=0):
        _row_copy(src_hbm, idx_ref[base + t], dst, t, chunks, pitch, sem).start(priority=priority)
    if unroll:
        for t in range(n):
            start(t, t % 2 if alternate else 0)
    else:
        lax.fori_loop(0, n, lambda t, c: (start(t), c)[1], 0)


def _wait_rows(src_hbm, n, dst, chunks, pitch, sem):
    for t in range(n):
        _row_copy(src_hbm, 0, dst, t, chunks, pitch, sem).wait()


def _moe_up_kernel(tok_ref, be_ref, nu_ref, u_hbm, wg_ref, wu_ref, bg_ref, bu_ref, h_ref,
                   buf, xs_ref, sem, *, tm, chunks, pitch, n_blocks):
    r = pl.program_id(0)
    n_used = nu_ref[0]
    slot = r % 2
    rows = functools.partial(_start_rows, u_hbm, tok_ref, n=tm, chunks=chunks, pitch=pitch)
    wait = lambda s: _wait_rows(u_hbm, tm, buf.at[s], chunks, pitch, sem.at[s])

    @pl.when(r == 0)
    def _():
        rows(base=0, dst=buf.at[0], sem=sem.at[0], unroll=False)

    @pl.when(r < n_used)
    def _():
        wait(slot)
        xs_ref[...] = _read_chunks(buf, (slot,), tm, chunks, pitch).astype(BF16)
        rows(base=(r + 1) * tm, dst=buf.at[1 - slot], sem=sem.at[1 - slot], unroll=True)

    @pl.when(r >= n_used)
    def _():
        h_ref[...] = jnp.zeros_like(h_ref)

    @pl.when(r + 1 <= n_used)
    def _():
        x = xs_ref[...]
        gate = jnp.dot(x, wg_ref[0], preferred_element_type=F32) + bg_ref[0]
        up = jnp.dot(x, wu_ref[0], preferred_element_type=F32) + bu_ref[0]
        gate = jnp.minimum(gate, SWIGLU_LIMIT)
        up = jnp.clip(up, -SWIGLU_LIMIT, SWIGLU_LIMIT)
        h_ref[...] = ((up + 1.0) * gate * jax.nn.sigmoid(SWIGLU_ALPHA * gate)).astype(BF16)

        @pl.when(r == n_blocks - 1)
        def _():
            wait(1 - slot)

    @pl.when(r == n_used)
    def _():
        wait(slot)


def _moe_up(row_tok, block_e, n_used, u3, w_gu, b_gu, chunks, n_rows, tm):
    d = chunks * LANES
    f = w_gu.shape[2] // 2
    pitch = _pitch(chunks)
    n_blocks = n_rows // tm
    grid_spec = pltpu.PrefetchScalarGridSpec(
        num_scalar_prefetch=3,
        grid=(n_blocks,),
        in_specs=[pl.BlockSpec(memory_space=pl.ANY),
                  pl.BlockSpec((1, d, f), lambda r, tok, be, nu: (be[r], 0, 0)),
                  pl.BlockSpec((1, d, f), lambda r, tok, be, nu: (be[r], 0, 1)),
                  pl.BlockSpec((1, 1, f), lambda r, tok, be, nu: (be[r], 0, 0)),
                  pl.BlockSpec((1, 1, f), lambda r, tok, be, nu: (be[r], 0, 1))],
        out_specs=pl.BlockSpec((tm, f), lambda r, tok, be, nu: (r, 0)),
        scratch_shapes=[pltpu.VMEM((2, tm * pitch, LANES), F32), pltpu.VMEM((tm, d), BF16),
                        pltpu.SemaphoreType.DMA((2,))])
    return pl.pallas_call(
        functools.partial(_moe_up_kernel, tm=tm, chunks=chunks, pitch=pitch, n_blocks=n_blocks),
        grid_spec=grid_spec,
        out_shape=jax.ShapeDtypeStruct((n_rows, f), BF16),
        compiler_params=_cparams(("arbitrary",)),
        name="moe_up",
    )(row_tok, block_e, n_used, u3, w_gu, w_gu, b_gu, b_gu)


def _moe_down_kernel(be_ref, nu_ref, h_ref, wd_ref, bd_ref, y_ref, *, tm, chunks):
    r = pl.program_id(0)

    @pl.when(r < nu_ref[0])
    def _():
        y = jnp.dot(h_ref[...], wd_ref[0], preferred_element_type=F32) + bd_ref[0]
        _write_chunks(y_ref, y, tm, chunks)

    @pl.when(r >= nu_ref[0])
    def _():
        y_ref[...] = jnp.zeros_like(y_ref)


def _moe_down(block_e, n_used, hid, w_d, b_d, tm):
    n_rows, f = hid.shape
    d = w_d.shape[2]
    chunks = d // LANES
    grid_spec = pltpu.PrefetchScalarGridSpec(
        num_scalar_prefetch=2,
        grid=(n_rows // tm,),
        in_specs=[pl.BlockSpec((tm, f), lambda r, be, nu: (r, 0)),
                  pl.BlockSpec((1, f, d), lambda r, be, nu: (be[r], 0, 0)),
                  pl.BlockSpec((1, 1, d), lambda r, be, nu: (be[r], 0, 0))],
        out_specs=pl.BlockSpec((tm * chunks, LANES), lambda r, be, nu: (r, 0)))
    return pl.pallas_call(
        functools.partial(_moe_down_kernel, tm=tm, chunks=chunks),
        grid_spec=grid_spec,
        out_shape=jax.ShapeDtypeStruct((n_rows * chunks, LANES), F32),
        compiler_params=_cparams(("arbitrary",)),
        name="moe_down",
    )(block_e, n_used, hid, w_d, b_d)


def _combine_kernel(pos_ref, ys_hbm, h_ref, gw_ref, ple_ref, wpg_ref, wpp_ref, gple_ref, gfin_ref,
                    y_ref, buf, hs_ref, sem, *, tm, chunks, pitch):
    i = pl.program_id(0)
    slot = i % 2

    def start(block, s, unroll):
        for k in range(TOP_K):
            _start_rows(ys_hbm, pos_ref, (block * TOP_K + k) * tm, tm, buf.at[s, k], chunks, pitch,
                        sem.at[s], unroll, alternate=True)

    def wait(s):
        for k in range(TOP_K):
            _wait_rows(ys_hbm, tm, buf.at[s, k], chunks, pitch, sem.at[s])

    @pl.when(i == 0)
    def _():
        start(0, 0, False)

    wait(slot)
    gw = gw_ref[...]
    h = h_ref[...]
    for k in range(TOP_K):
        h = h + gw[:, k:k + 1] * _read_chunks(buf, (slot, k), tm, chunks, pitch)
    hs_ref[...] = h
    start(i + 1, 1 - slot, True)

    @pl.when(i >= 0)
    def _():
        h = hs_ref[...]
        u = _rms(h, gple_ref[...]).astype(BF16)
        gate = jax.nn.sigmoid(jnp.dot(u, wpg_ref[...], preferred_element_type=F32))
        proj = jnp.dot(ple_ref[...].astype(BF16), wpp_ref[...], preferred_element_type=F32)
        y_ref[...] = _rms(h + gate * proj, gfin_ref[...])

    @pl.when(i == pl.num_programs(0) - 1)
    def _():
        wait(1 - slot)


def _combine(pos, ys3, h, gw, ple, w_pg, w_pp, g_ple, g_fin, tm):
    n, d = h.shape
    chunks = d // LANES
    pitch = _pitch(chunks)
    row = lambda w: pl.BlockSpec((tm, w), lambda i, p: (i, 0))
    const = lambda a: _const_spec(a.shape)
    grid_spec = pltpu.PrefetchScalarGridSpec(
        num_scalar_prefetch=1,
        grid=(n // tm,),
        in_specs=[pl.BlockSpec(memory_space=pl.ANY), row(d), row(LANES), row(ple.shape[1]),
                  const(w_pg), const(w_pp), const(g_ple), const(g_fin)],
        out_specs=row(d),
        scratch_shapes=[pltpu.VMEM((2, TOP_K, tm * pitch, LANES), F32), pltpu.VMEM((tm, d), F32),
                        pltpu.SemaphoreType.DMA((2,))])
    return pl.pallas_call(
        functools.partial(_combine_kernel, tm=tm, chunks=chunks, pitch=pitch),
        grid_spec=grid_spec,
        out_shape=jax.ShapeDtypeStruct((n, d), F32),
        compiler_params=_cparams(("arbitrary",)),
        name="moe_combine",
    )(pos.reshape(-1), ys3, h, gw, ple, w_pg, w_pp, g_ple, g_fin)


def _route(top_idx, n_experts, tm):
    n_tok = top_idx.shape[0]
    n_slot = n_tok * TOP_K
    n_rows = -(-(n_slot + n_experts * (tm - 1)) // tm) * tm
    flat_e = top_idx.reshape(-1)
    onehot = (flat_e[:, None] == jnp.arange(n_experts, dtype=jnp.int32)[None, :]).astype(jnp.int32)
    csum = jnp.cumsum(onehot, axis=0)
    rank = jnp.sum(onehot * csum, axis=1) - 1
    counts = csum[-1]
    padded = (counts + tm - 1) // tm * tm
    pad_ends = jnp.cumsum(padded)
    pad_starts = pad_ends - padded
    dest = (jnp.sum(onehot * pad_starts[None, :], axis=1) + rank).astype(jnp.int32)
    flat_tok = jnp.repeat(jnp.arange(n_tok, dtype=jnp.int32), TOP_K)
    row_tok = jnp.zeros((n_rows + tm,), jnp.int32).at[dest].set(flat_tok)
    block_start = jnp.arange(0, n_rows, tm, dtype=jnp.int32)
    block_e = jnp.minimum(jnp.sum(pad_ends[None, :] <= block_start[:, None], axis=1),
                          n_experts - 1).astype(jnp.int32)
    n_used = (pad_ends[-1:] // tm).astype(jnp.int32)
    return dest, row_tok, block_e, n_used, n_rows


def _pick(n, prefs):
    for p in prefs:
        if n % p == 0:
            return p
    return n


def _trunk(x, ple, attend, W, lam_init):
    n, d = x.shape
    tm = _pick(n, (512, 256, 128))
    q_sb, kv_sb, q_df, kv_df, gate = _norm_proj(x, W['norm_mix'], W['w_in'], tm)
    o_sb, o_df = attend(q_sb, kv_sb, q_df, kv_df)

    tm = _pick(n, (256, 128))
    n_experts = W['w_router'].shape[1]
    h1, u3, top_idx, gw = _post_attn(o_sb, o_df, gate, x, W['w_br_sb'], W['w_br_df'], W['w_out'],
                                     W['norm_moe'], W['w_router'], W['b_router'], tm)

    n_slot = n * TOP_K
    avg = max(1, n_slot // n_experts)
    tm_moe = max(16, min(256, 1 << (avg.bit_length() - 1)))
    dest, row_tok, block_e, n_used, n_rows = _route(top_idx[:, :TOP_K], n_experts, tm_moe)
    hid = _moe_up(row_tok, block_e, n_used, u3, W['w_gate_up'], W['b_gate_up'], d // LANES,
                  n_rows, tm_moe)
    ys3 = _moe_down(block_e, n_used, hid, W['w_down'], W['b_down'], tm_moe)
    tm_c = _pick(n, (128,))
    pos = dest.reshape(n // tm_c, tm_c, TOP_K).swapaxes(1, 2)
    pos = jnp.concatenate([pos, jnp.zeros((1, TOP_K, tm_c), jnp.int32)], axis=0)
    y = _combine(pos, ys3, h1, gw, ple, W['w_ple_gate'], W['w_ple_proj'], W['norm_ple'],
                 W['norm_final'], tm_c)
    return y, kv_sb, kv_df


def kernel(x_prompt, x_sample, cache_sb, cache_df, page_table, p_prompt, p_sample, norm_mix, w_in,
           df_lambda, df_subln, w_br_sb, w_br_df, w_out, norm_moe, w_router, b_router, w_gate_up,
           b_gate_up, w_down, b_down, norm_ple, w_ple_gate, w_ple_proj, norm_final):
    depth = w_in.shape[0]
    assert depth == 1
    b, t, d = x_prompt.shape
    bs, ts, _ = x_sample.shape
    dh = d // 16
    lam_init = _lambda_init(0)
    n_exp = w_router.shape[2]
    W = dict(
        norm_mix=norm_mix[0].reshape(1, d), w_in=w_in[0].astype(BF16),
        w_br_sb=w_br_sb[0].astype(BF16), w_br_df=w_br_df[0].astype(BF16),
        w_out=w_out[0].astype(BF16), norm_moe=norm_moe[0].reshape(1, d),
        w_router=w_router[0], b_router=b_router[0].reshape(1, n_exp),
        w_gate_up=w_gate_up[0].astype(BF16),
        b_gate_up=b_gate_up[0].reshape(n_exp, 1, -1),
        w_down=w_down[0].astype(BF16), b_down=b_down[0].reshape(n_exp, 1, d),
        norm_ple=norm_ple[0].reshape(1, d), w_ple_gate=w_ple_gate[0].astype(BF16),
        w_ple_proj=w_ple_proj[0].astype(BF16), norm_final=norm_final.reshape(1, d))
    slopes = jnp.exp2(-8.0 * jnp.arange(1, DF_HEADS + 1, dtype=F32) / DF_HEADS).reshape(1, DF_HEADS)
    lam_w = df_lambda[0]
    g_sub = df_subln[0].reshape(1, 2 * dh)

    def attend_prompt(q_sb, kv_sb, q_df, kv_df):
        tq = _pick(t, (256, 128))
        o_sb = _sb_prompt(q_sb.reshape(b, t, -1), kv_sb.reshape(b, t, -1), b, t, tq)
        o_df = _df_prompt(q_df.reshape(b, t, -1), kv_df.reshape(b, t, -1), slopes, lam_w, g_sub,
                          b, t, tq, lam_init)
        return o_sb.reshape(b * t, -1), o_df.reshape(b * t, -1)

    def attend_sample(q_sb, kv_sb, q_df, kv_df):
        return _sample_attn(q_sb, kv_sb, q_df, kv_df, cache_sb[0], cache_df[0], page_table,
                            slopes, lam_w, g_sub, ts, lam_init)

    y_p, sb_p, df_p = _trunk(x_prompt.reshape(b * t, d), p_prompt[0].reshape(b * t, -1),
                             attend_prompt, W, lam_init)
    y_s, sb_s, df_s = _trunk(x_sample.reshape(bs * ts, d), p_sample[0].reshape(bs * ts, -1),
                             attend_sample, W, lam_init)
    return (y_p.reshape(b, t, d), y_s.reshape(bs, ts, d),
            sb_p.reshape(depth, b, t, 2, SB_HEADS, dh),
            df_p.reshape(depth, b, t, 2, DF_HEADS, 2 * dh),
            sb_s.reshape(depth, bs, ts, 2, SB_HEADS, dh),
            df_s.reshape(depth, bs, ts, 2, DF_HEADS, 2 * dh))
```

```python
import functools
import math

import jax
import jax.numpy as jnp
from jax import lax
from jax.experimental import pallas as pl
from jax.experimental.pallas import tpu as pltpu

F32 = jnp.float32
BF16 = jnp.bfloat16

SB_HEADS = 8
DF_HEADS = 4
N_GROUPS = 8
TOP_K = 4
SWIGLU_LIMIT = 7.0
SWIGLU_ALPHA = 1.702
RMS_EPS = 1e-6
LANES = 128
NEG_BIG = -1e30
VMEM_LIMIT = 56 * 1024 * 1024


def _lambda_init(i):
    return 0.8 - 0.6 * math.exp(-0.3 * i)


def _cparams(sem, vmem=VMEM_LIMIT):
    return pltpu.CompilerParams(dimension_semantics=sem, vmem_limit_bytes=vmem)


def _rms(x, g):
    return x * lax.rsqrt(jnp.mean(x * x, axis=-1, keepdims=True) + RMS_EPS) * g


def _const_spec(shape):
    nd = len(shape)
    return pl.BlockSpec(shape, lambda *_: (0,) * nd, pipeline_mode=pl.Buffered(1))


def _pitch(chunks):
    return chunks + 4 if chunks % 8 == 0 else chunks


def _read_chunks(ref, lead, rows, chunks, pitch):
    return jnp.concatenate(
        [ref[lead + (pl.ds(c, rows, stride=pitch), slice(None))] for c in range(chunks)], axis=1)


def _write_chunks(ref, val, rows, chunks):
    for c in range(chunks):
        ref[pl.ds(c, rows, stride=chunks), :] = val[:, c * LANES:(c + 1) * LANES]


def _norm_proj_kernel(x_ref, g_ref, w_ref, qsb_ref, kvsb_ref, qdf_ref, kvdf_ref, gate_ref,
                      u_ref, *, sb_scale, df_scale):
    j = pl.program_id(1)

    @pl.when(j == 0)
    def _():
        u_ref[...] = _rms(x_ref[...], g_ref[...]).astype(BF16)

    acc = jnp.dot(u_ref[...], w_ref[...], preferred_element_type=F32)

    @pl.when(j == 0)
    def _():
        qsb_ref[...] = (acc * sb_scale).astype(BF16)

    @pl.when((j >= 1) & (j < 3))
    def _():
        kvsb_ref[...] = acc

    @pl.when(j == 3)
    def _():
        qdf_ref[...] = (acc * df_scale).astype(BF16)

    @pl.when((j >= 4) & (j < 6))
    def _():
        kvdf_ref[...] = acc

    @pl.when(j >= 6)
    def _():
        gate_ref[...] = jax.nn.sigmoid(acc).astype(BF16)


def _norm_proj(x, g, w, tm):
    n, d = x.shape
    tn = d // 2
    n_in = w.shape[1]
    assert n_in == 10 * tn and n % tm == 0
    dh = d // 16
    kern = functools.partial(_norm_proj_kernel, sb_scale=1.0 / math.sqrt(dh),
                             df_scale=1.0 / math.sqrt(dh))
    clip = lambda j, lo, n_t: jnp.clip(j - lo, 0, n_t - 1)
    return pl.pallas_call(
        kern,
        grid=(n // tm, 10),
        in_specs=[pl.BlockSpec((tm, d), lambda i, j: (i, 0)),
                  pl.BlockSpec((1, d), lambda i, j: (0, 0)),
                  pl.BlockSpec((d, tn), lambda i, j: (0, j))],
        out_specs=[pl.BlockSpec((tm, tn), lambda i, j: (i, 0)),
                   pl.BlockSpec((tm, tn), lambda i, j: (i, clip(j, 1, 2))),
                   pl.BlockSpec((tm, tn), lambda i, j: (i, 0)),
                   pl.BlockSpec((tm, tn), lambda i, j: (i, clip(j, 4, 2))),
                   pl.BlockSpec((tm, tn), lambda i, j: (i, clip(j, 6, 4)))],
        out_shape=[jax.ShapeDtypeStruct((n, tn), BF16),
                   jax.ShapeDtypeStruct((n, 2 * tn), F32),
                   jax.ShapeDtypeStruct((n, tn), BF16),
                   jax.ShapeDtypeStruct((n, 2 * tn), F32),
                   jax.ShapeDtypeStruct((n, 4 * tn), BF16)],
        scratch_shapes=[pltpu.VMEM((tm, d), BF16)],
        compiler_params=_cparams(("parallel", "arbitrary")),
        name="norm_proj",
    )(x, g, w)


def _softplus(z):
    return jnp.maximum(z, 0.0) + jnp.log(1.0 + jnp.exp(-jnp.abs(z)))


def _sb_block(q, k, v, tri2, sub, mask, carry, acc):
    z = lax.dot_general(q, k, (((1,), (1,)), ((), ())), preferred_element_type=F32)
    lf = -_softplus(z)
    if mask is not None:
        lf = jnp.where(mask, lf, 0.0)
    after = []
    for s in reversed(range(k.shape[0] // sub)):
        lfs = lf[:, s * sub:(s + 1) * sub]
        hi = lfs.astype(BF16)
        lo = (lfs - hi.astype(F32)).astype(BF16)
        inner = jnp.dot(jnp.concatenate([hi, lo], axis=1), tri2, preferred_element_type=F32)
        after.append(inner + carry)
        carry = carry + jnp.sum(lfs, axis=-1, keepdims=True)
    after = after[0] if len(after) == 1 else jnp.concatenate(after[::-1], axis=1)
    a = jnp.exp(z + lf + after)
    if mask is not None:
        a = jnp.where(mask, a, 0.0)
    acc = acc + jnp.dot(a.astype(BF16), v, preferred_element_type=F32)
    return carry, acc


def _df_block(q, k, v, bias, mask, m, l, acc):
    s = lax.dot_general(q, k, (((1,), (1,)), ((), ())), preferred_element_type=F32) - bias
    if mask is not None:
        s = jnp.where(mask, s, NEG_BIG)
    m_new = jnp.maximum(m, jnp.max(s, axis=-1, keepdims=True))
    alpha = jnp.exp(m - m_new)
    p = jnp.exp(s - m_new)
    l = alpha * l + jnp.sum(p, axis=-1, keepdims=True)
    acc = alpha * acc + jnp.dot(p.astype(BF16), v, preferred_element_type=F32)
    return m_new, l, acc


def _df_lambda(lam_ref, lam_init):
    lq1, lk1 = lam_ref[0:1, :], lam_ref[1:2, :]
    lq2, lk2 = lam_ref[2:3, :], lam_ref[3:4, :]
    s1 = jnp.sum(lq1 * lk1, axis=-1, keepdims=True)
    s2 = jnp.sum(lq2 * lk2, axis=-1, keepdims=True)
    return jnp.exp(s1) - jnp.exp(s2) + lam_init


def _tri2(sub):
    r = lax.broadcasted_iota(jnp.int32, (2 * sub, sub), 0)
    c = lax.broadcasted_iota(jnp.int32, (2 * sub, sub), 1)
    return ((r > c) & ((r < sub) | (r > c + sub))).astype(BF16)


def _side_cast(w, grid):
    n_steps = grid[0] * grid[1] * grid[2]
    w2 = w.reshape(-1, w.shape[-1])
    rows = w2.shape[0] // n_steps
    assert rows * n_steps == w2.shape[0] and rows % 16 == 0
    spec = pl.BlockSpec((rows, w2.shape[1]),
                        lambda a, b, c: ((a * grid[1] + b) * grid[2] + c, 0))
    return w2, spec, jax.ShapeDtypeStruct(w2.shape, BF16)


def _sb_prompt_kernel(q_ref, k_ref, v_ref, wi_ref, o_ref, wo_ref, *, tq):
    wo_ref[...] = wi_ref[...].astype(BF16)
    i = pl.program_id(2)
    hps = SB_HEADS_PER_STEP
    dh = q_ref.shape[-1] // hps
    head = lambda a, h: a[:, h * dh:(h + 1) * dh]
    q = q_ref[0]
    tri2 = _tri2(tq)
    row = lax.broadcasted_iota(jnp.int32, (tq, tq), 0)
    col = lax.broadcasted_iota(jnp.int32, (tq, tq), 1)

    def tile(kb, mask, state):
        start = pl.multiple_of(kb * tq, tq)
        k = k_ref[0, pl.ds(start, tq), :].astype(BF16)
        v = v_ref[0, pl.ds(start, tq), :].astype(BF16)
        return tuple(_sb_block(head(q, h), head(k, h), head(v, h), tri2, tq, mask, *state[h])
                     for h in range(hps))

    zero = (jnp.zeros((tq, 1), F32), jnp.zeros((tq, dh), F32))
    init = tile(i, col < row, (zero,) * hps)
    state = lax.fori_loop(0, i, lambda n, c: tile(i - 1 - n, None, c), init)
    o_ref[0] = jnp.concatenate([acc for _, acc in state], axis=1).astype(o_ref.dtype)


SB_HEADS_PER_STEP = 2


def _sb_prompt(q, kv, w_f32, b, t, tq):
    hps = SB_HEADS_PER_STEP
    dh = hps * (q.shape[-1] // SB_HEADS)
    grid = (b, SB_HEADS // hps, t // tq)
    w2, w_spec, w_out = _side_cast(w_f32, grid)
    o, w_bf16 = pl.pallas_call(
        functools.partial(_sb_prompt_kernel, tq=tq),
        grid=grid,
        in_specs=[pl.BlockSpec((1, tq, dh), lambda bb, h, i: (bb, i, h)),
                  pl.BlockSpec((1, t, dh), lambda bb, h, i: (bb, 0, h)),
                  pl.BlockSpec((1, t, dh), lambda bb, h, i: (bb, 0, SB_HEADS // hps + h)),
                  w_spec],
        out_specs=[pl.BlockSpec((1, tq, dh), lambda bb, h, i: (bb, i, h)), w_spec],
        out_shape=[jax.ShapeDtypeStruct(q.shape, BF16), w_out],
        compiler_params=_cparams(("parallel", "parallel", "arbitrary")),
        name="sb_prompt",
    )(q, kv, kv, w2)
    return o, w_bf16.reshape(w_f32.shape)


def _df_finish(o1, o2, lam, g, lam_init):
    o = o1 - lam * o2
    return _rms(o, g) * (1.0 - lam_init)


def _df_prompt_kernel(slope_ref, q_ref, k_ref, v_ref, lam_ref, g_ref, wi_ref, o_ref, wo_ref,
                      *, tq, lam_init):
    wo_ref[...] = wi_ref[...].astype(BF16)
    h = pl.program_id(1)
    i = pl.program_id(2)
    dh = q_ref.shape[-1] // 2
    q1 = q_ref[0, :, :dh]
    q2 = q_ref[0, :, dh:]
    slope = slope_ref[0, h]
    row = lax.broadcasted_iota(jnp.int32, (tq, tq), 0)
    col = lax.broadcasted_iota(jnp.int32, (tq, tq), 1)
    rel = slope * (row - col).astype(F32)

    def tile(kb, mask, c):
        m1, l1, a1, m2, l2, a2 = c
        start = pl.multiple_of(kb * tq, tq)
        k1 = k_ref[0, pl.ds(start, tq), :dh].astype(BF16)
        k2 = k_ref[0, pl.ds(start, tq), dh:].astype(BF16)
        v = v_ref[0, pl.ds(start, tq), :].astype(BF16)
        bias = rel + slope * ((i - kb) * tq).astype(F32)
        m1, l1, a1 = _df_block(q1, k1, v, bias, mask, m1, l1, a1)
        m2, l2, a2 = _df_block(q2, k2, v, bias, mask, m2, l2, a2)
        return m1, l1, a1, m2, l2, a2

    zc = jnp.zeros((tq, 1), F32)
    za = jnp.zeros((tq, 2 * dh), F32)
    init = tile(i, col <= row, (zc + NEG_BIG, zc, za, zc + NEG_BIG, zc, za))
    _, l1, a1, _, l2, a2 = lax.fori_loop(0, i, lambda kb, c: tile(kb, None, c), init)
    lam = _df_lambda(lam_ref, lam_init)
    o_ref[0] = _df_finish(a1 / l1, a2 / l2, lam, g_ref[...], lam_init).astype(o_ref.dtype)


def _df_prompt(q, kv, slopes, df_lambda, g, w_f32, b, t, tq, lam_init):
    w = q.shape[-1] // DF_HEADS
    grid = (b, DF_HEADS, t // tq)
    w2, w_spec, w_out = _side_cast(w_f32, grid)
    o, w_bf16 = pl.pallas_call(
        functools.partial(_df_prompt_kernel, tq=tq, lam_init=lam_init),
        grid=grid,
        in_specs=[pl.BlockSpec(memory_space=pltpu.SMEM),
                  pl.BlockSpec((1, tq, w), lambda bb, h, i: (bb, i, h)),
                  pl.BlockSpec((1, t, w), lambda bb, h, i: (bb, 0, h)),
                  pl.BlockSpec((1, t, w), lambda bb, h, i: (bb, 0, DF_HEADS + h)),
                  _const_spec(df_lambda.shape),
                  _const_spec(g.shape),
                  w_spec],
        out_specs=[pl.BlockSpec((1, tq, w), lambda bb, h, i: (bb, i, h)), w_spec],
        out_shape=[jax.ShapeDtypeStruct(q.shape, BF16), w_out],
        compiler_params=_cparams(("parallel", "parallel", "arbitrary")),
        name="df_prompt",
    )(slopes, q, kv, kv, df_lambda, g, w2)
    return o, w_bf16.reshape(w_f32.shape)


def _sample_attn_kernel(pt_ref, qsb_ref, qdf_ref, nsb_ref, ndf_ref, *rest,
                        n_pages, page, pps, t_new, lam_init):
    csb_refs, cdf_refs = rest[:pps], rest[pps:2 * pps]
    (slope_ref, lam_ref, g_ref, osb_ref, odf_ref,
     carry_ref, asb_ref, m_ref, l_ref, adf_ref) = rest[2 * pps:]
    j = pl.program_id(1)
    n_rows = qsb_ref.shape[1]
    width = qsb_ref.shape[2]
    past = n_pages * page
    tri2 = _tri2(page)

    @pl.when(j == 0)
    def _():
        carry_ref[...] = jnp.zeros_like(carry_ref)
        asb_ref[...] = jnp.zeros_like(asb_ref)
        m_ref[...] = jnp.full_like(m_ref, NEG_BIG)
        l_ref[...] = jnp.zeros_like(l_ref)
        adf_ref[...] = jnp.zeros_like(adf_ref)

    def step(ksb, vsb, kdf, vdf, kbase, masked):
        tk = ksb.shape[0]
        row = lax.broadcasted_iota(jnp.int32, (n_rows, tk), 0)
        col = lax.broadcasted_iota(jnp.int32, (n_rows, tk), 1)
        qpos = past + row % t_new
        kpos = kbase + col
        carry, asb = _sb_block(qsb_ref[0], ksb, vsb, tri2, page, (kpos < qpos) if masked else None,
                               carry_ref[...], asb_ref[...])
        carry_ref[...] = carry
        asb_ref[...] = asb
        bias = slope_ref[...] * (qpos - kpos).astype(F32)
        m, l, adf = _df_block(qdf_ref[0], kdf, vdf, bias, (kpos <= qpos) if masked else None,
                              m_ref[...], l_ref[...], adf_ref[...])
        m_ref[...] = m
        l_ref[...] = l
        adf_ref[...] = adf

    def groups(ref2d, kv):
        per = 2 * N_GROUPS
        return jnp.concatenate(
            [ref2d[pl.ds(kv * N_GROUPS + g, page, stride=per), :] for g in range(N_GROUPS)],
            axis=1).astype(BF16)

    @pl.when(j == 0)
    def _():
        pad = jnp.zeros((page - t_new, width), F32)
        nsb = nsb_ref[...]
        ndf = ndf_ref[...]
        ksb = jnp.concatenate([nsb[:, :width], pad], axis=0).astype(BF16)
        vsb = jnp.concatenate([nsb[:, width:], pad], axis=0).astype(BF16)
        kdf = jnp.concatenate([ndf[:, :width], pad], axis=0).astype(BF16)
        vdf = jnp.concatenate([ndf[:, width:], pad], axis=0).astype(BF16)
        step(ksb, vsb, kdf, vdf, past, True)

    @pl.when(j > 0)
    def _():
        cat = lambda refs, kv: jnp.concatenate([groups(r.at[0], kv) for r in refs], axis=0)
        step(cat(csb_refs, 0), cat(csb_refs, 1), cat(cdf_refs, 0), cat(cdf_refs, 1),
             (n_pages - j * pps) * page, False)

    @pl.when(j == n_pages // pps)
    def _():
        dh = width // N_GROUPS
        asb = asb_ref[...]
        osb_ref[...] = jnp.concatenate(
            [asb[h * t_new:(h + 1) * t_new, h * dh:(h + 1) * dh] for h in range(SB_HEADS)], axis=1)
        adf = adf_ref[...] / l_ref[...]
        lam = _df_lambda(lam_ref, lam_init)
        outs = []
        for h in range(DF_HEADS):
            def pick(m, h=h):
                rows = adf[(m * DF_HEADS + h) * t_new:(m * DF_HEADS + h + 1) * t_new, :]
                return jnp.concatenate([rows[:, h * dh:(h + 1) * dh],
                                        rows[:, (DF_HEADS + h) * dh:(DF_HEADS + h + 1) * dh]],
                                       axis=1)
            outs.append(_df_finish(pick(0), pick(1), lam, g_ref[...], lam_init))
        odf_ref[...] = jnp.concatenate(outs, axis=1)


def _block_diag_q(q, t_new):
    n, width = q.shape
    b = n // t_new
    q3 = q.reshape(b, 1, t_new, width)
    grp_r = jnp.arange(N_GROUPS)[:, None, None]
    grp_c = (jnp.arange(width) // (width // N_GROUPS))[None, None, :]
    out = jnp.where(grp_r == grp_c, q3, jnp.zeros((), q.dtype))
    return out.reshape(b, N_GROUPS * t_new, width)


def _sample_attn(q_sb, kv_sb, q_df, kv_df, cache_sb, cache_df, page_table, slopes, df_lambda, g,
                 t_new, lam_init):
    n, width = q_sb.shape
    b = n // t_new
    dh = width // N_GROUPS
    n_pages = page_table.shape[1]
    n_pool, page = cache_sb.shape[0], cache_sb.shape[1]
    csb = cache_sb.reshape(n_pool, page * 2 * N_GROUPS, dh)
    cdf = cache_df.reshape(n_pool, page, 2, DF_HEADS, 2, dh).transpose(0, 1, 2, 4, 3, 5)
    cdf = cdf.reshape(n_pool, page * 2 * N_GROUPS, dh)
    half_major = lambda a, lead: a.reshape(lead + (DF_HEADS, 2, dh)).swapaxes(-2, -3).reshape(
        lead + (N_GROUPS * dh,))
    q_df = half_major(q_df, (n,))
    kv_df = half_major(kv_df, (n, 2)).reshape(n, 2 * width)
    n_rows = N_GROUPS * t_new
    slope_rows = jnp.tile(jnp.repeat(slopes.reshape(-1), t_new), 2).reshape(n_rows, 1)
    pt = page_table.reshape(-1).astype(jnp.int32)

    pps = _pick(n_pages, (8, 4, 2))
    if pps == n_pages and n_pages > 8:
        pps = 1

    def page_idx(s):
        return lambda bb, j, pt_ref: (
            pt_ref[bb * n_pages + n_pages - jnp.maximum(j, 1) * pps + s], 0, 0)

    kern = functools.partial(_sample_attn_kernel, n_pages=n_pages, page=page, pps=pps,
                             t_new=t_new, lam_init=lam_init)
    grid_spec = pltpu.PrefetchScalarGridSpec(
        num_scalar_prefetch=1,
        grid=(b, n_pages // pps + 1),
        in_specs=[pl.BlockSpec((1, n_rows, width), lambda bb, j, p: (bb, 0, 0)),
                  pl.BlockSpec((1, n_rows, width), lambda bb, j, p: (bb, 0, 0)),
                  pl.BlockSpec((t_new, 2 * width), lambda bb, j, p: (bb, 0)),
                  pl.BlockSpec((t_new, 2 * width), lambda bb, j, p: (bb, 0))]
                 + [pl.BlockSpec((1,) + csb.shape[1:], page_idx(s)) for s in range(pps)]
                 + [pl.BlockSpec((1,) + cdf.shape[1:], page_idx(s)) for s in range(pps)]
                 + [pl.BlockSpec((n_rows, 1), lambda bb, j, p: (0, 0)),
                  pl.BlockSpec(df_lambda.shape, lambda bb, j, p: (0, 0)),
                  pl.BlockSpec(g.shape, lambda bb, j, p: (0, 0))],
        out_specs=[pl.BlockSpec((t_new, width), lambda bb, j, p: (bb, 0)),
                   pl.BlockSpec((t_new, width), lambda bb, j, p: (bb, 0))],
        scratch_shapes=[pltpu.VMEM((n_rows, 1), F32), pltpu.VMEM((n_rows, width), F32),
                        pltpu.VMEM((n_rows, 1), F32), pltpu.VMEM((n_rows, 1), F32),
                        pltpu.VMEM((n_rows, width), F32)])
    return pl.pallas_call(
        kern,
        grid_spec=grid_spec,
        out_shape=[jax.ShapeDtypeStruct((n, width), F32), jax.ShapeDtypeStruct((n, width), F32)],
        compiler_params=_cparams(("parallel", "arbitrary")),
        name="sample_attn",
    )(pt, _block_diag_q(q_sb, t_new), _block_diag_q(q_df, t_new), kv_sb, kv_df,
      *([csb] * pps), *([cdf] * pps), slope_rows, df_lambda, g)


def _post_attn_kernel(osb_ref, odf_ref, gate_ref, x_ref, wsb_ref, wdf_ref, wout_ref, gmoe_ref,
                      wr_ref, br_ref, h_ref, u_ref, idx_ref, gw_ref, *, n_experts):
    d = x_ref.shape[-1]
    br_sb = jnp.dot(osb_ref[...].astype(BF16), wsb_ref[...], preferred_element_type=F32)
    br_df = jnp.dot(odf_ref[...].astype(BF16), wdf_ref[...], preferred_element_type=F32)
    gate = gate_ref[...].astype(F32)
    merged = gate[:, :d] * br_sb + gate[:, d:] * br_df
    h = x_ref[...] + jnp.dot(merged.astype(BF16), wout_ref[...], preferred_element_type=F32)
    h_ref[...] = h
    u = _rms(h, gmoe_ref[...])
    _write_chunks(u_ref, u, u.shape[0], d // LANES)
    logits = jnp.dot(u, wr_ref[...], preferred_element_type=F32,
                     precision=lax.Precision.HIGHEST) + br_ref[...]
    tm = logits.shape[0]
    lane = lax.broadcasted_iota(jnp.int32, (tm, n_experts), 1)
    out_lane = lax.broadcasted_iota(jnp.int32, (tm, LANES), 1)
    idx_out = jnp.zeros((tm, LANES), jnp.int32)
    val_out = jnp.zeros((tm, LANES), F32)
    top = None
    denom = jnp.zeros((tm, 1), F32)
    for k in range(TOP_K):
        mx = jnp.max(logits, axis=-1, keepdims=True)
        idx = jnp.min(jnp.where(logits == mx, lane, n_experts), axis=-1, keepdims=True)
        logits = jnp.where(lane == idx, -jnp.inf, logits)
        if top is None:
            top = mx
        e = jnp.exp(mx - top)
        denom = denom + e
        idx_out = jnp.where(out_lane == k, idx, idx_out)
        val_out = jnp.where(out_lane == k, e, val_out)
    idx_ref[...] = idx_out
    gw_ref[...] = val_out / denom


def _post_attn(o_sb, o_df, gate, x, w_sb, w_df, w_out, g_moe, w_router, b_router, tm):
    n, d = x.shape
    e = w_router.shape[1]
    row = lambda w: pl.BlockSpec((tm, w), lambda i: (i, 0))
    return pl.pallas_call(
        functools.partial(_post_attn_kernel, n_experts=e),
        grid=(n // tm,),
        in_specs=[row(o_sb.shape[1]), row(o_df.shape[1]), row(gate.shape[1]), row(d),
                  _const_spec(w_sb.shape), _const_spec(w_df.shape), _const_spec(w_out.shape),
                  _const_spec(g_moe.shape), _const_spec(w_router.shape),
                  _const_spec(b_router.shape)],
        out_specs=[row(d),
                   pl.BlockSpec((tm * (d // LANES), LANES), lambda i: (i, 0)),
                   row(LANES), row(LANES)],
        out_shape=[jax.ShapeDtypeStruct((n, d), F32),
                   jax.ShapeDtypeStruct((n * (d // LANES), LANES), F32),
                   jax.ShapeDtypeStruct((n, LANES), jnp.int32),
                   jax.ShapeDtypeStruct((n, LANES), F32)],
        compiler_params=_cparams(("parallel",)),
        name="post_attn",
    )(o_sb, o_df, gate, x, w_sb, w_df, w_out, g_moe, w_router, b_router)


def _row_copy(src_hbm, src_row, dst, dst_slot, chunks, pitch, sem):
    src = src_hbm.at[pl.ds(pl.multiple_of(src_row * chunks, chunks), chunks), :]
    return pltpu.make_async_copy(src, dst.at[pl.ds(dst_slot * pitch, chunks), :], sem)


def _start_rows(src_hbm, idx_ref, base, n, dst, chunks, pitch, sem, unroll, alternate=False):
    def start(t, priority=0):
        _row_copy(src_hbm, idx_ref[base + t], dst, t, chunks, pitch, sem).start(priority=priority)
    if unroll:
        for t in range(n):
            start(t, t % 2 if alternate else 0)
    else:
        lax.fori_loop(0, n, lambda t, c: (start(t), c)[1], 0)


def _wait_rows(src_hbm, n, dst, chunks, pitch, sem):
    for t in range(n):
        _row_copy(src_hbm, 0, dst, t, chunks, pitch, sem).wait()


def _moe_up_kernel(tok_ref, be_ref, nu_ref, u_hbm, wg_ref, wu_ref, bg_ref, bu_ref, h_ref,
                   buf, xs_ref, sem, *, tm, chunks, pitch, n_blocks):
    r = pl.program_id(0)
    n_used = nu_ref[0]
    slot = r % 2
    rows = functools.partial(_start_rows, u_hbm, tok_ref, n=tm, chunks=chunks, pitch=pitch)
    wait = lambda s: _wait_rows(u_hbm, tm, buf.at[s], chunks, pitch, sem.at[s])

    @pl.when(r == 0)
    def _():
        rows(base=0, dst=buf.at[0], sem=sem.at[0], unroll=False)

    @pl.when(r < n_used)
    def _():
        wait(slot)
        xs_ref[...] = _read_chunks(buf, (slot,), tm, chunks, pitch).astype(BF16)
        rows(base=(r + 1) * tm, dst=buf.at[1 - slot], sem=sem.at[1 - slot], unroll=True)

    @pl.when(r >= n_used)
    def _():
        h_ref[...] = jnp.zeros_like(h_ref)

    @pl.when(r + 1 <= n_used)
    def _():
        x = xs_ref[...]
        gate = jnp.dot(x, wg_ref[0], preferred_element_type=F32) + bg_ref[0]
        up = jnp.dot(x, wu_ref[0], preferred_element_type=F32) + bu_ref[0]
        gate = jnp.minimum(gate, SWIGLU_LIMIT)
        up = jnp.clip(up, -SWIGLU_LIMIT, SWIGLU_LIMIT)
        h_ref[...] = ((up + 1.0) * gate * jax.nn.sigmoid(SWIGLU_ALPHA * gate)).astype(BF16)

        @pl.when(r == n_blocks - 1)
        def _():
            wait(1 - slot)

    @pl.when(r == n_used)
    def _():
        wait(slot)


def _moe_up(row_tok, block_e, n_used, u3, w_gu, b_gu, chunks, n_rows, tm):
    d = chunks * LANES
    f = w_gu.shape[2] // 2
    pitch = _pitch(chunks)
    n_blocks = n_rows // tm
    grid_spec = pltpu.PrefetchScalarGridSpec(
        num_scalar_prefetch=3,
        grid=(n_blocks,),
        in_specs=[pl.BlockSpec(memory_space=pl.ANY),
                  pl.BlockSpec((1, d, f), lambda r, tok, be, nu: (be[r], 0, 0)),
                  pl.BlockSpec((1, d, f), lambda r, tok, be, nu: (be[r], 0, 1)),
                  pl.BlockSpec((1, 1, f), lambda r, tok, be, nu: (be[r], 0, 0)),
                  pl.BlockSpec((1, 1, f), lambda r, tok, be, nu: (be[r], 0, 1))],
        out_specs=pl.BlockSpec((tm, f), lambda r, tok, be, nu: (r, 0)),
        scratch_shapes=[pltpu.VMEM((2, tm * pitch, LANES), F32), pltpu.VMEM((tm, d), BF16),
                        pltpu.SemaphoreType.DMA((2,))])
    return pl.pallas_call(
        functools.partial(_moe_up_kernel, tm=tm, chunks=chunks, pitch=pitch, n_blocks=n_blocks),
        grid_spec=grid_spec,
        out_shape=jax.ShapeDtypeStruct((n_rows, f), BF16),
        compiler_params=_cparams(("arbitrary",)),
        name="moe_up",
    )(row_tok, block_e, n_used, u3, w_gu, w_gu, b_gu, b_gu)


def _moe_down_kernel(be_ref, nu_ref, h_ref, wd_ref, bd_ref, y_ref, *, tm, chunks):
    r = pl.program_id(0)

    @pl.when(r < nu_ref[0])
    def _():
        y = jnp.dot(h_ref[...], wd_ref[0], preferred_element_type=F32) + bd_ref[0]
        _write_chunks(y_ref, y, tm, chunks)

    @pl.when(r >= nu_ref[0])
    def _():
        y_ref[...] = jnp.zeros_like(y_ref)


def _moe_down(block_e, n_used, hid, w_d, b_d, tm):
    n_rows, f = hid.shape
    d = w_d.shape[2]
    chunks = d // LANES
    grid_spec = pltpu.PrefetchScalarGridSpec(
        num_scalar_prefetch=2,
        grid=(n_rows // tm,),
        in_specs=[pl.BlockSpec((tm, f), lambda r, be, nu: (r, 0)),
                  pl.BlockSpec((1, f, d), lambda r, be, nu: (be[r], 0, 0)),
                  pl.BlockSpec((1, 1, d), lambda r, be, nu: (be[r], 0, 0))],
        out_specs=pl.BlockSpec((tm * chunks, LANES), lambda r, be, nu: (r, 0)))
    return pl.pallas_call(
        functools.partial(_moe_down_kernel, tm=tm, chunks=chunks),
        grid_spec=grid_spec,
        out_shape=jax.ShapeDtypeStruct((n_rows * chunks, LANES), F32),
        compiler_params=_cparams(("arbitrary",)),
        name="moe_down",
    )(block_e, n_used, hid, w_d, b_d)


def _combine_kernel(pos_ref, ys_hbm, h_ref, gw_ref, ple_ref, wpg_ref, wpp_ref, gple_ref, gfin_ref,
                    y_ref, buf, hs_ref, sem, *, tm, chunks, pitch):
    i = pl.program_id(0)
    slot = i % 2

    def start(block, s, unroll):
        for k in range(TOP_K):
            _start_rows(ys_hbm, pos_ref, (block * TOP_K + k) * tm, tm, buf.at[s, k], chunks, pitch,
                        sem.at[s], unroll, alternate=True)

    def wait(s):
        for k in range(TOP_K):
            _wait_rows(ys_hbm, tm, buf.at[s, k], chunks, pitch, sem.at[s])

    @pl.when(i == 0)
    def _():
        start(0, 0, False)

    wait(slot)
    gw = gw_ref[...]
    h = h_ref[...]
    for k in range(TOP_K):
        h = h + gw[:, k:k + 1] * _read_chunks(buf, (slot, k), tm, chunks, pitch)
    hs_ref[...] = h
    start(i + 1, 1 - slot, True)

    @pl.when(i >= 0)
    def _():
        h = hs_ref[...]
        u = _rms(h, gple_ref[...]).astype(BF16)
        gate = jax.nn.sigmoid(jnp.dot(u, wpg_ref[...], preferred_element_type=F32))
        proj = jnp.dot(ple_ref[...].astype(BF16), wpp_ref[...], preferred_element_type=F32)
        y_ref[...] = _rms(h + gate * proj, gfin_ref[...])

    @pl.when(i == pl.num_programs(0) - 1)
    def _():
        wait(1 - slot)


def _combine(pos, ys3, h, gw, ple, w_pg, w_pp, g_ple, g_fin, tm):
    n, d = h.shape
    chunks = d // LANES
    pitch = _pitch(chunks)
    row = lambda w: pl.BlockSpec((tm, w), lambda i, p: (i, 0))
    const = lambda a: _const_spec(a.shape)
    grid_spec = pltpu.PrefetchScalarGridSpec(
        num_scalar_prefetch=1,
        grid=(n // tm,),
        in_specs=[pl.BlockSpec(memory_space=pl.ANY), row(d), row(LANES), row(ple.shape[1]),
                  const(w_pg), const(w_pp), const(g_ple), const(g_fin)],
        out_specs=row(d),
        scratch_shapes=[pltpu.VMEM((2, TOP_K, tm * pitch, LANES), F32), pltpu.VMEM((tm, d), F32),
                        pltpu.SemaphoreType.DMA((2,))])
    return pl.pallas_call(
        functools.partial(_combine_kernel, tm=tm, chunks=chunks, pitch=pitch),
        grid_spec=grid_spec,
        out_shape=jax.ShapeDtypeStruct((n, d), F32),
        compiler_params=_cparams(("arbitrary",)),
        name="moe_combine",
    )(pos.reshape(-1), ys3, h, gw, ple, w_pg, w_pp, g_ple, g_fin)


def _route(top_idx, n_experts, tm):
    n_tok = top_idx.shape[0]
    n_slot = n_tok * TOP_K
    n_rows = -(-(n_slot + n_experts * (tm - 1)) // tm) * tm
    flat_e = top_idx.reshape(-1)
    onehot = (flat_e[:, None] == jnp.arange(n_experts, dtype=jnp.int32)[None, :]).astype(jnp.int32)
    csum = jnp.cumsum(onehot, axis=0)
    rank = jnp.sum(onehot * csum, axis=1) - 1
    counts = csum[-1]
    padded = (counts + tm - 1) // tm * tm
    pad_ends = jnp.cumsum(padded)
    pad_starts = pad_ends - padded
    dest = (jnp.sum(onehot * pad_starts[None, :], axis=1) + rank).astype(jnp.int32)
    flat_tok = jnp.repeat(jnp.arange(n_tok, dtype=jnp.int32), TOP_K)
    row_tok = jnp.zeros((n_rows + tm,), jnp.int32).at[dest].set(flat_tok)
    block_start = jnp.arange(0, n_rows, tm, dtype=jnp.int32)
    block_e = jnp.minimum(jnp.sum(pad_ends[None, :] <= block_start[:, None], axis=1),
                          n_experts - 1).astype(jnp.int32)
    n_used = (pad_ends[-1:] // tm).astype(jnp.int32)
    return dest, row_tok, block_e, n_used, n_rows


def _pick(n, prefs):
    for p in prefs:
        if n % p == 0:
            return p
    return n


def _trunk(x, ple, attend, W, lam_init):
    n, d = x.shape
    tm = _pick(n, (512, 256, 128))
    q_sb, kv_sb, q_df, kv_df, gate = _norm_proj(x, W['norm_mix'], W['w_in'], tm)
    o_sb, o_df = attend(q_sb, kv_sb, q_df, kv_df)

    tm = _pick(n, (256, 128))
    n_experts = W['w_router'].shape[1]
    h1, u3, top_idx, gw = _post_attn(o_sb, o_df, gate, x, W['w_br_sb'], W['w_br_df'], W['w_out'],
                                     W['norm_moe'], W['w_router'], W['b_router'], tm)

    n_slot = n * TOP_K
    avg = max(1, n_slot // n_experts)
    tm_moe = max(16, min(256, 1 << (avg.bit_length() - 1)))
    dest, row_tok, block_e, n_used, n_rows = _route(top_idx[:, :TOP_K], n_experts, tm_moe)
    hid = _moe_up(row_tok, block_e, n_used, u3, W['w_gate_up'], W['b_gate_up'], d // LANES,
                  n_rows, tm_moe)
    ys3 = _moe_down(block_e, n_used, hid, W['w_down'], W['b_down'], tm_moe)
    tm_c = _pick(n, (128,))
    pos = dest.reshape(n // tm_c, tm_c, TOP_K).swapaxes(1, 2)
    pos = jnp.concatenate([pos, jnp.zeros((1, TOP_K, tm_c), jnp.int32)], axis=0)
    y = _combine(pos, ys3, h1, gw, ple, W['w_ple_gate'], W['w_ple_proj'], W['norm_ple'],
                 W['norm_final'], tm_c)
    return y, kv_sb, kv_df


def kernel(x_prompt, x_sample, cache_sb, cache_df, page_table, p_prompt, p_sample, norm_mix, w_in,
           df_lambda, df_subln, w_br_sb, w_br_df, w_out, norm_moe, w_router, b_router, w_gate_up,
           b_gate_up, w_down, b_down, norm_ple, w_ple_gate, w_ple_proj, norm_final):
    depth = w_in.shape[0]
    assert depth == 1
    b, t, d = x_prompt.shape
    bs, ts, _ = x_sample.shape
    dh = d // 16
    lam_init = _lambda_init(0)
    n_exp = w_router.shape[2]
    W = dict(
        norm_mix=norm_mix[0].reshape(1, d), w_in=w_in[0].astype(BF16),
        w_br_sb=w_br_sb[0].astype(BF16), w_br_df=w_br_df[0].astype(BF16),
        w_out=w_out[0].astype(BF16), norm_moe=norm_moe[0].reshape(1, d),
        w_router=w_router[0], b_router=b_router[0].reshape(1, n_exp),
        b_gate_up=b_gate_up[0].reshape(n_exp, 1, -1), b_down=b_down[0].reshape(n_exp, 1, d),
        norm_ple=norm_ple[0].reshape(1, d), w_ple_gate=w_ple_gate[0].astype(BF16),
        w_ple_proj=w_ple_proj[0].astype(BF16), norm_final=norm_final.reshape(1, d))
    slopes = jnp.exp2(-8.0 * jnp.arange(1, DF_HEADS + 1, dtype=F32) / DF_HEADS).reshape(1, DF_HEADS)
    lam_w = df_lambda[0]
    g_sub = df_subln[0].reshape(1, 2 * dh)

    def attend_prompt(q_sb, kv_sb, q_df, kv_df):
        tq = _pick(t, (256, 128))
        o_sb, W['w_gate_up'] = _sb_prompt(q_sb.reshape(b, t, -1), kv_sb.reshape(b, t, -1),
                                          w_gate_up[0], b, t, tq)
        o_df, W['w_down'] = _df_prompt(q_df.reshape(b, t, -1), kv_df.reshape(b, t, -1), slopes,
                                       lam_w, g_sub, w_down[0], b, t, tq, lam_init)
        return o_sb.reshape(b * t, -1), o_df.reshape(b * t, -1)

    def attend_sample(q_sb, kv_sb, q_df, kv_df):
        return _sample_attn(q_sb, kv_sb, q_df, kv_df, cache_sb[0], cache_df[0], page_table,
                            slopes, lam_w, g_sub, ts, lam_init)

    y_p, sb_p, df_p = _trunk(x_prompt.reshape(b * t, d), p_prompt[0].reshape(b * t, -1),
                             attend_prompt, W, lam_init)
    y_s, sb_s, df_s = _trunk(x_sample.reshape(bs * ts, d), p_sample[0].reshape(bs * ts, -1),
                             attend_sample, W, lam_init)
    return (y_p.reshape(b, t, d), y_s.reshape(bs, ts, d),
            sb_p.reshape(depth, b, t, 2, SB_HEADS, dh),
            df_p.reshape(depth, b, t, 2, DF_HEADS, 2 * dh),
            sb_s.reshape(depth, bs, ts, 2, SB_HEADS, dh),
            df_s.reshape(depth, bs, ts, 2, DF_HEADS, 2 * dh))
```
